```python
import jax, jax.numpy as jnp
from jax import lax
import numpy as np

D_MODEL = 1024
BATCH = 4
SEQ = 8192
DEPTH = 4

N_MIXERS = 2
EPS = 1e-6

SSM_EXPAND = 2
SSM_D_INNER = SSM_EXPAND * D_MODEL
SSM_HEAD_DIM = 64
SSM_HEADS = SSM_D_INNER // SSM_HEAD_DIM
SSM_GROUPS = 8
SSM_HEADS_PER_GROUP = SSM_HEADS // SSM_GROUPS
SSM_STATE = 128
SSM_CONV = 4
SSM_CHUNK = 128
SSM_BC_DIM = SSM_GROUPS * SSM_STATE
SSM_CONV_DIM = SSM_D_INNER + 2 * SSM_BC_DIM
SSM_IN_DIM = SSM_D_INNER + SSM_CONV_DIM + SSM_HEADS

ATT_HEAD_DIM = 64
ATT_Q_HEADS = D_MODEL // ATT_HEAD_DIM
ATT_KV_HEADS = 4
ATT_GQA = ATT_Q_HEADS // ATT_KV_HEADS
ATT_WIDTH = ATT_Q_HEADS * ATT_HEAD_DIM
ATT_KV_WIDTH = ATT_KV_HEADS * ATT_HEAD_DIM
ATT_IN_DIM = 2 * ATT_WIDTH + 2 * ATT_KV_WIDTH
WINDOW = 128
ATT_BLOCK = 128
ROPE_THETA = 500000.0
ROPE_DIM = ATT_HEAD_DIM // 4

N_SSM_LAYERS = (DEPTH + 1) // 2
N_ATT_LAYERS = DEPTH // 2

kernel_name = "hybrid_ssd_swa_sink_trunk"


def rmsnorm(x, w):
    xf = x.astype(jnp.float32)
    y = xf * lax.rsqrt(jnp.mean(xf * xf, axis=-1, keepdims=True) + EPS)
    return (y * w.astype(jnp.float32)).astype(x.dtype)


def causal_depthwise_conv(u, w, b):
    c = u.shape[-1]
    out = lax.conv_general_dilated(
        u, w[:, None, :].astype(u.dtype), window_strides=(1,),
        padding=((SSM_CONV - 1, 0),), dimension_numbers=('NWC', 'WIO', 'NWC'),
        feature_group_count=c)
    return out + b.astype(u.dtype)


def ssd_chunked(xs, dt, A, Bm, Cm):
    b, L = xs.shape[:2]
    c, l = L // SSM_CHUNK, SSM_CHUNK
    G, R, P, N = SSM_GROUPS, SSM_HEADS_PER_GROUP, SSM_HEAD_DIM, SSM_STATE
    x = (xs * dt[..., None]).reshape(b, c, l, G, R, P)
    a = jnp.moveaxis((dt * A).reshape(b, c, l, G, R), 2, -1)
    a_cs = jnp.cumsum(a, axis=-1)
    Bc = Bm.reshape(b, c, l, G, N)
    Cc = Cm.reshape(b, c, l, G, N)
    causal = jnp.tril(jnp.ones((l, l), dtype=bool))
    seg = a_cs[..., :, None] - a_cs[..., None, :]
    decay = jnp.exp(jnp.where(causal, seg, -jnp.inf))
    cb = jnp.einsum('bclgn,bcsgn->bcgls', Cc, Bc)
    y_diag = jnp.einsum('bcgrls,bcsgrp->bclgrp', cb[:, :, :, None] * decay, x)
    decay_to_end = jnp.exp(a_cs[..., -1:] - a_cs)
    states = jnp.einsum('bclgn,bcgrl,bclgrp->bcgrpn', Bc, decay_to_end, x)
    chunk_decay = jnp.exp(a_cs[..., -1])

    def step(h, inp):
        s, d = inp
        return h * d[..., None, None] + s, h

    h0 = jnp.zeros_like(states[:, 0])
    _, h_in = lax.scan(step, h0, (jnp.moveaxis(states, 1, 0), jnp.moveaxis(chunk_decay, 1, 0)))
    h_in = jnp.moveaxis(h_in, 0, 1)
    y_off = jnp.einsum('bclgn,bcgrpn,bcgrl->bclgrp', Cc, h_in, jnp.exp(a_cs))
    return (y_diag + y_off).reshape(b, L, SSM_HEADS, P)


def mamba2_mixer(h, w_in, conv_w, conv_b, dt_bias, a_log, d_skip, gate_norm, w_out):
    b, L, _ = h.shape
    f32 = jnp.float32
    z, xbc, dt = jnp.split(h @ w_in, [SSM_D_INNER, SSM_D_INNER + SSM_CONV_DIM], axis=-1)
    xbc = jax.nn.silu(causal_depthwise_conv(xbc, conv_w, conv_b))
    xs, Bm, Cm = jnp.split(xbc, [SSM_D_INNER, SSM_D_INNER + SSM_BC_DIM], axis=-1)
    xs = xs.reshape(b, L, SSM_HEADS, SSM_HEAD_DIM).astype(f32)
    Bm = Bm.reshape(b, L, SSM_GROUPS, SSM_STATE).astype(f32)
    Cm = Cm.reshape(b, L, SSM_GROUPS, SSM_STATE).astype(f32)
    dt = jax.nn.softplus(dt.astype(f32) + dt_bias.astype(f32))
    A = -jnp.exp(a_log.astype(f32))
    y = ssd_chunked(xs, dt, A, Bm, Cm) + d_skip.astype(f32)[:, None] * xs
    y = y.reshape(b, L, SSM_D_INNER) * jax.nn.silu(z.astype(f32))
    y = rmsnorm(y, gate_norm)
    return y.astype(h.dtype) @ w_out


def rope_tables(positions):
    inv = ROPE_THETA ** (-jnp.arange(0, ROPE_DIM, 2, dtype=jnp.float32) / ROPE_DIM)
    ang = positions.astype(jnp.float32)[..., None] * inv
    return jnp.cos(ang)[:, :, None, :], jnp.sin(ang)[:, :, None, :]


def apply_partial_rope(t, cos, sin):
    half = ROPE_DIM // 2
    t1, t2, rest = t[..., :half], t[..., half:ROPE_DIM], t[..., ROPE_DIM:]
    return jnp.concatenate([t1 * cos - t2 * sin, t2 * cos + t1 * sin, rest], axis=-1)


def swa_sink_mixer(h, cos, sin, w_in, sinks, w_out):
    b, L, _ = h.shape
    f32 = jnp.float32
    Hk, D, BLK = ATT_KV_HEADS, ATT_HEAD_DIM, ATT_BLOCK
    q, k, v, gate = jnp.split(
        h @ w_in, [ATT_WIDTH, ATT_WIDTH + ATT_KV_WIDTH, ATT_WIDTH + 2 * ATT_KV_WIDTH], axis=-1)
    q = apply_partial_rope(q.reshape(b, L, ATT_Q_HEADS, D).astype(f32), cos, sin)
    k = apply_partial_rope(k.reshape(b, L, Hk, D).astype(f32), cos, sin)
    v = v.reshape(b, L, Hk, D).astype(f32)
    nb = L // BLK
    qb = q.reshape(b, nb, BLK, Hk, ATT_GQA, D)

    def band(t):
        tp = jnp.pad(t, ((0, 0), (BLK, 0), (0, 0), (0, 0))).reshape(b, nb + 1, BLK, Hk, D)
        return jnp.concatenate([tp[:, :-1], tp[:, 1:]], axis=2)

    kb, vb = band(k), band(v)
    s = jnp.einsum('bnikgd,bnjkd->bnkgij', qb, kb) * (D ** -0.5)
    qi = jnp.arange(BLK)[:, None]
    kj = jnp.arange(2 * BLK)[None, :]
    dist = qi + BLK - kj
    blk = jnp.arange(nb)[:, None, None]
    valid = (dist >= 0) & (dist < WINDOW) & ((blk - 1) * BLK + kj >= 0)
    s = jnp.where(valid[None, :, None, None], s, -jnp.inf)
    sink = sinks.astype(f32).reshape(Hk, ATT_GQA)[None, None, :, :, None, None]
    m = jnp.maximum(jnp.max(s, axis=-1, keepdims=True), sink)
    p = jnp.exp(s - m)
    denom = jnp.sum(p, axis=-1, keepdims=True) + jnp.exp(sink - m)
    o = jnp.einsum('bnkgij,bnjkd->bnikgd', p / denom, vb)
    o = o.reshape(b, L, ATT_WIDTH) * jax.nn.silu(gate.astype(f32))
    return o.astype(h.dtype) @ w_out


def setup_inputs(seed: int = 0) -> dict:
    key = jax.random.key(seed)
    ks = jax.random.split(key, 16)
    f32 = jnp.float32
    nS, nA = N_SSM_LAYERS, N_ATT_LAYERS
    x = jax.random.normal(ks[0], (BATCH, SEQ, D_MODEL), f32)
    positions = jnp.broadcast_to(jnp.arange(SEQ, dtype=jnp.int32), (BATCH, SEQ))
    pre_norm = 1.0 + 0.1 * jax.random.normal(ks[1], (DEPTH, D_MODEL), f32)
    post_norm = 1.0 + 0.1 * jax.random.normal(ks[2], (DEPTH, D_MODEL), f32)
    ssm_w_in = jax.random.normal(ks[3], (nS, D_MODEL, SSM_IN_DIM), f32) * D_MODEL ** -0.5
    ssm_conv_w = jax.random.normal(ks[4], (nS, SSM_CONV, SSM_CONV_DIM), f32) * SSM_CONV ** -0.5
    ssm_conv_b = 0.02 * jax.random.normal(ks[5], (nS, SSM_CONV_DIM), f32)
    dt0 = jnp.exp(jax.random.uniform(ks[6], (nS, SSM_HEADS), f32)
                  * (np.log(0.1) - np.log(0.001)) + np.log(0.001)).astype(f32)
    ssm_dt_bias = dt0 + jnp.log(-jnp.expm1(-dt0))
    ssm_a_log = jnp.log(jax.random.uniform(ks[7], (nS, SSM_HEADS), f32, 1.0, 16.0))
    ssm_d = 1.0 + 0.1 * jax.random.normal(ks[8], (nS, SSM_HEADS), f32)
    ssm_gate_norm = 1.0 + 0.1 * jax.random.normal(ks[9], (nS, SSM_D_INNER), f32)
    ssm_w_out = jax.random.normal(ks[10], (nS, SSM_D_INNER, D_MODEL), f32) * SSM_D_INNER ** -0.5
    att_w_in = jax.random.normal(ks[11], (nA, D_MODEL, ATT_IN_DIM), f32) * D_MODEL ** -0.5
    att_sinks = 0.5 * jax.random.normal(ks[12], (nA, ATT_Q_HEADS), f32)
    att_w_out = jax.random.normal(ks[13], (nA, ATT_WIDTH, D_MODEL), f32) * ATT_WIDTH ** -0.5
    return {"x": x, "positions": positions, "pre_norm": pre_norm, "post_norm": post_norm,
            "ssm_w_in": ssm_w_in, "ssm_conv_w": ssm_conv_w, "ssm_conv_b": ssm_conv_b,
            "ssm_dt_bias": ssm_dt_bias, "ssm_a_log": ssm_a_log, "ssm_d": ssm_d,
            "ssm_gate_norm": ssm_gate_norm, "ssm_w_out": ssm_w_out,
            "att_w_in": att_w_in, "att_sinks": att_sinks, "att_w_out": att_w_out}


def reference(x, positions, pre_norm, post_norm, ssm_w_in, ssm_conv_w, ssm_conv_b,
              ssm_dt_bias, ssm_a_log, ssm_d, ssm_gate_norm, ssm_w_out,
              att_w_in, att_sinks, att_w_out):
    cos, sin = rope_tables(positions)
    for i in range(DEPTH):
        h = rmsnorm(x, pre_norm[i])
        j = i // N_MIXERS
        if i % N_MIXERS == 0:
            y = mamba2_mixer(h, ssm_w_in[j], ssm_conv_w[j], ssm_conv_b[j], ssm_dt_bias[j],
                             ssm_a_log[j], ssm_d[j], ssm_gate_norm[j], ssm_w_out[j])
        else:
            y = swa_sink_mixer(h, cos, sin, att_w_in[j], att_sinks[j], att_w_out[j])
        x = x + rmsnorm(y, post_norm[i])
    return x
```

```python
import functools

import jax
import jax.numpy as jnp
from jax import lax
from jax.experimental import pallas as pl
from jax.experimental.pallas import tpu as pltpu

f32 = jnp.float32
bf16 = jnp.bfloat16

EPS = 1e-6
LANES = 128
SUBLANES = 8
VMEM_LIMIT_BYTES = 56 * 1024 * 1024

D_MODEL = 1024
SSM_D_INNER = 2048
SSM_HEAD_DIM = 64
SSM_HEADS = 32
SSM_GROUPS = 8
SSM_STATE = 128
SSM_CONV = 4
SSM_CHUNK = 128
SSM_BC_DIM = SSM_GROUPS * SSM_STATE
SSM_CONV_DIM = SSM_D_INNER + 2 * SSM_BC_DIM
ATT_HEAD_DIM = 64
ATT_Q_HEADS = 16
ATT_KV_HEADS = 4
ATT_WIDTH = 1024
ATT_KV_WIDTH = 256
ATT_BLOCK = 128
ROPE_THETA = 500000.0
ROPE_DIM = 16

SSD_TILE = 256
SWA_TILE = 256

NT_DIMS = (((1,), (1,)), ((), ()))
TN_DIMS = (((0,), (0,)), ((), ()))


def _dot(a, b):
    return jnp.dot(a, b, preferred_element_type=f32)


def _rms_scale(v):
    return lax.rsqrt(jnp.sum(v * v, axis=-1, keepdims=True) * (1.0 / v.shape[-1]) + EPS)


def _silu(v):
    return v * jax.nn.sigmoid(v)


def _ssd_kernel(x_ref, prew_ref, postw_ref, wz_ref, wxbc_ref, wdt_ref, cw_ref, cb_ref,
                dtb_ref, alog_ref, dsk_ref, gn_ref, wout_ref, o_ref,
                cbuf, xs_scr, b_scr, c_scr, y_scr, h_scr, state_ref):
    ts = SSD_TILE
    nch = ts // SSM_CHUNK
    t = pl.program_id(1)

    @pl.when(t == 0)
    def _():
        cbuf[:, 0:SUBLANES, :] = jnp.zeros((cbuf.shape[0], SUBLANES, LANES), f32)
        state_ref[...] = jnp.zeros(state_ref.shape, f32)

    x = x_ref[0]
    h_scr[...] = (x * _rms_scale(x) * prew_ref[...]).astype(bf16)

    for j in range(SSM_CONV_DIM // 256):
        raw = _dot(h_scr[...], wxbc_ref[:, 256 * j:256 * (j + 1)])
        for half in range(2):
            i = 2 * j + half
            cols = slice(LANES * i, LANES * (i + 1))
            cbuf[i, SUBLANES:SUBLANES + ts, :] = raw[:, LANES * half:LANES * (half + 1)]
            acc = cb_ref[:, cols] + cbuf[i, pl.ds(SUBLANES - 3, ts), :] * cw_ref[0:1, cols]
            for k in range(1, SSM_CONV):
                acc = acc + cbuf[i, pl.ds(SUBLANES - 3 + k, ts), :] * cw_ref[k:k + 1, cols]
            act = _silu(acc)
            cbuf[i, 0:SUBLANES, :] = cbuf[i, ts:ts + SUBLANES, :]
            if i < SSM_D_INNER // LANES:
                xs_scr[:, cols] = act
            elif i < (SSM_D_INNER + SSM_BC_DIM) // LANES:
                b_scr[:, LANES * i - SSM_D_INNER:LANES * (i + 1) - SSM_D_INNER] = act.astype(bf16)
            else:
                off = SSM_D_INNER + SSM_BC_DIM
                c_scr[:, LANES * i - off:LANES * (i + 1) - off] = act.astype(bf16)

    dt = jax.nn.softplus(_dot(h_scr[...], wdt_ref[...]) + dtb_ref[...])
    a = dt * (-jnp.exp(alog_ref[...]))
    r_i = lax.broadcasted_iota(jnp.int32, (SSM_CHUNK, SSM_CHUNK), 0)
    c_i = lax.broadcasted_iota(jnp.int32, (SSM_CHUNK, SSM_CHUNK), 1)
    causal = r_i >= c_i
    ltri = causal.astype(f32)
    lo = c_i < SSM_HEAD_DIM
    acs_l, acst_l, dtt_l, eacs_l, w_l, cd_l = [], [], [], [], [], []
    for c in range(nch):
        rows = slice(SSM_CHUNK * c, SSM_CHUNK * (c + 1))
        acs = jnp.dot(ltri, a[rows], precision=lax.Precision.HIGHEST, preferred_element_type=f32)
        last = acs[SSM_CHUNK - 1:SSM_CHUNK, :]
        acs_l.append(acs)
        acst_l.append(acs.T)
        dtt_l.append(dt[rows].T)
        eacs_l.append(jnp.exp(acs))
        w_l.append(dt[rows] * jnp.exp(last - acs))
        cd_l.append(jnp.exp(last))

    lane256 = lax.broadcasted_iota(jnp.int32, (1, 2 * LANES), 1) // SSM_HEAD_DIM
    ssq = [jnp.zeros((SSM_CHUNK, LANES), f32) for _ in range(nch)]
    for g in range(SSM_GROUPS):
        z_g = _dot(h_scr[...], wz_ref[:, 256 * g:256 * (g + 1)])
        s_g = state_ref[g]
        for c in range(nch):
            rows = slice(SSM_CHUNK * c, SSM_CHUNK * (c + 1))
            b_g = b_scr[rows, LANES * g:LANES * (g + 1)]
            c_g = c_scr[rows, LANES * g:LANES * (g + 1)]
            cbm = lax.dot_general(c_g, b_g, NT_DIMS, preferred_element_type=f32)
            yoff = _dot(c_g, s_g.astype(bf16))
            xd_parts = []
            for q in range(2):
                h0 = 4 * g + 2 * q
                cols = slice(256 * g + LANES * q, 256 * g + LANES * (q + 1))
                xp = xs_scr[rows, cols]
                xpb = xp.astype(bf16)
                ms = []
                for hh in (h0, h0 + 1):
                    seg = acs_l[c][:, hh:hh + 1] - acst_l[c][hh:hh + 1, :]
                    dec = jnp.exp(jnp.where(causal, seg, -jnp.inf))
                    ms.append((cbm * dec * dtt_l[c][hh:hh + 1, :]).astype(bf16))
                zb = jnp.zeros_like(xpb)
                bd = jnp.concatenate([jnp.where(lo, xpb, zb), jnp.where(lo, zb, xpb)], axis=0)
                ydiag = _dot(jnp.concatenate(ms, axis=1), bd)
                e_pair = jnp.where(lo, eacs_l[c][:, h0:h0 + 1], eacs_l[c][:, h0 + 1:h0 + 2])
                w_pair = jnp.where(lo, w_l[c][:, h0:h0 + 1], w_l[c][:, h0 + 1:h0 + 2])
                y = ydiag + yoff[:, LANES * q:LANES * (q + 1)] * e_pair + dsk_ref[:, cols] * xp
                yg = y * _silu(z_g[rows, LANES * q:LANES * (q + 1)])
                y_scr[rows, cols] = yg
                ssq[c] = ssq[c] + yg * yg
                xd_parts.append((xp * w_pair).astype(bf16))
            xd = jnp.concatenate(xd_parts, axis=1)
            cd_row = jnp.zeros((1, 2 * LANES), f32)
            for r in range(4):
                cd_row = jnp.where(lane256 == r, cd_l[c][:, 4 * g + r:4 * g + r + 1], cd_row)
            s_g = s_g * cd_row + lax.dot_general(b_g, xd, TN_DIMS, preferred_element_type=f32)
        state_ref[g] = s_g

    ssq_all = jnp.concatenate(ssq, axis=0)
    rms = lax.rsqrt(jnp.sum(ssq_all, axis=-1, keepdims=True) * (1.0 / SSM_D_INNER) + EPS)
    yn = (y_scr[...] * rms * gn_ref[...]).astype(bf16)
    out = _dot(yn, wout_ref[...])
    o_ref[0] = x + out * _rms_scale(out) * postw_ref[...]


def _const_spec(shape):
    nd = len(shape)
    return pl.BlockSpec(shape, lambda b, t: (0,) * nd, pipeline_mode=pl.Buffered(1))


def _ssd_layer(x, prew, postw, w_in, conv_w, conv_b, dt_bias, a_log, d_skip, gate_norm, w_out):
    bsz, seq, d = x.shape
    ts = SSD_TILE
    wz = w_in[:, :SSM_D_INNER].astype(bf16)
    wxbc = w_in[:, SSM_D_INNER:SSM_D_INNER + SSM_CONV_DIM].astype(bf16)
    wdt = w_in[:, SSM_D_INNER + SSM_CONV_DIM:].astype(bf16)
    args = (x, prew.reshape(1, d), postw.reshape(1, d), wz, wxbc, wdt,
            conv_w, conv_b.reshape(1, -1), dt_bias.reshape(1, -1), a_log.reshape(1, -1),
            jnp.repeat(d_skip, SSM_HEAD_DIM).reshape(1, -1), gate_norm.reshape(1, -1),
            w_out.astype(bf16))
    x_spec = pl.BlockSpec((1, ts, d), lambda b, t: (b, t, 0))
    in_specs = [x_spec] + [_const_spec(a.shape) for a in args[1:]]
    return pl.pallas_call(
        _ssd_kernel,
        grid=(bsz, seq // ts),
        in_specs=in_specs,
        out_specs=x_spec,
        out_shape=jax.ShapeDtypeStruct(x.shape, f32),
        scratch_shapes=[
            pltpu.VMEM((SSM_CONV_DIM // LANES, ts + SUBLANES, LANES), f32),
            pltpu.VMEM((ts, SSM_D_INNER), f32),
            pltpu.VMEM((ts, SSM_BC_DIM), bf16),
            pltpu.VMEM((ts, SSM_BC_DIM), bf16),
            pltpu.VMEM((ts, SSM_D_INNER), f32),
            pltpu.VMEM((ts, d), bf16),
            pltpu.VMEM((SSM_GROUPS, SSM_STATE, 4 * SSM_HEAD_DIM), f32),
        ],
        compiler_params=pltpu.CompilerParams(
            dimension_semantics=("arbitrary", "arbitrary"),
            vmem_limit_bytes=VMEM_LIMIT_BYTES),
        name="ssd_layer",
    )(*args)


def _swa_kernel(sink_ref, x_ref, cos_ref, sin_ref, prew_ref, postw_ref, wq_ref, wk_ref, wv_ref,
                wg_ref, wout_ref, o_ref, kpad, vpad, q_scr, o_scr, h_scr):
    ts = SWA_TILE
    nqb = ts // ATT_BLOCK
    t = pl.program_id(1)

    @pl.when(t == 0)
    def _():
        kpad[:, 0:ATT_BLOCK, :] = jnp.zeros((kpad.shape[0], ATT_BLOCK, LANES), bf16)
        vpad[:, 0:ATT_BLOCK, :] = jnp.zeros((vpad.shape[0], ATT_BLOCK, LANES), bf16)

    x = x_ref[0]
    h_scr[...] = (x * _rms_scale(x) * prew_ref[...]).astype(bf16)

    cos_t = cos_ref[0]
    sin_t = sin_ref[0]
    l64 = lax.broadcasted_iota(jnp.int32, (ts, LANES), 1) % ATT_HEAD_DIM
    hi_half = lax.broadcasted_iota(jnp.int32, (ts, LANES), 1) >= ATT_HEAD_DIM

    def rope(tb, c, s):
        sw = jnp.where(l64 < ROPE_DIM // 2, pltpu.roll(tb, LANES - ROPE_DIM // 2, axis=1),
                       jnp.where(l64 < ROPE_DIM, pltpu.roll(tb, ROPE_DIM // 2, axis=1), 0.0))
        return tb * c + sw * s

    scale = ATT_HEAD_DIM ** -0.5
    cos_q = cos_t * scale
    sin_q = sin_t * scale
    for j in range(ATT_WIDTH // LANES):
        qb = _dot(h_scr[...], wq_ref[:, LANES * j:LANES * (j + 1)])
        q_scr[:, LANES * j:LANES * (j + 1)] = rope(qb, cos_q, sin_q).astype(bf16)

    k = _dot(h_scr[...], wk_ref[...])
    v = _dot(h_scr[...], wv_ref[...])
    new = slice(ATT_BLOCK, ATT_BLOCK + ts)
    for kb in range(ATT_KV_WIDTH // LANES):
        kblk = rope(k[:, LANES * kb:LANES * (kb + 1)], cos_t, sin_t)
        vblk = v[:, LANES * kb:LANES * (kb + 1)]
        ksw = pltpu.roll(kblk, ATT_HEAD_DIM, axis=1)
        vsw = pltpu.roll(vblk, ATT_HEAD_DIM, axis=1)
        for hk in range(2):
            kh = 2 * kb + hk
            for hq in range(2):
                ksrc = kblk if hq == hk else ksw
                vsrc = vblk if hq == hk else vsw
                keep = hi_half if hq == 1 else jnp.logical_not(hi_half)
                kpad[2 * kh + hq, new, :] = jnp.where(keep, ksrc, 0.0).astype(bf16)
                vpad[2 * kh + hq, new, :] = jnp.where(keep, vsrc, 0.0).astype(bf16)

    qi = lax.broadcasted_iota(jnp.int32, (ATT_BLOCK, 2 * ATT_BLOCK), 0)
    kj = lax.broadcasted_iota(jnp.int32, (ATT_BLOCK, 2 * ATT_BLOCK), 1)
    dist = qi + ATT_BLOCK - kj
    in_window = jnp.logical_and(dist >= 0, dist < ATT_BLOCK)
    first_ok = jnp.logical_and(in_window, jnp.logical_or(kj >= ATT_BLOCK, t > 0))
    lo = lax.broadcasted_iota(jnp.int32, (ATT_BLOCK, LANES), 1) < ATT_HEAD_DIM
    for i in range(nqb):
        rows = slice(ATT_BLOCK * i, ATT_BLOCK * (i + 1))
        keys = slice(ATT_BLOCK * i, ATT_BLOCK * (i + 2))
        valid = first_ok if i == 0 else in_window
        for j in range(ATT_WIDTH // LANES):
            kh = j // 2
            qblk = q_scr[rows, LANES * j:LANES * (j + 1)]
            ps, dens = [], []
            for hq in range(2):
                sink = sink_ref[2 * j + hq]
                s = lax.dot_general(qblk, kpad[2 * kh + hq, keys, :], NT_DIMS,
                                    preferred_element_type=f32)
                s = jnp.where(valid, s, -jnp.inf)
                m = jnp.maximum(jnp.max(s, axis=-1, keepdims=True), sink)
                p = jnp.exp(s - m)
                dens.append(jnp.sum(p, axis=-1, keepdims=True) + jnp.exp(sink - m))
                ps.append(p.astype(bf16))
            vcat = jnp.concatenate([vpad[2 * kh, keys, :], vpad[2 * kh + 1, keys, :]], axis=0)
            o = _dot(jnp.concatenate(ps, axis=1), vcat)
            o_scr[rows, LANES * j:LANES * (j + 1)] = o * jnp.where(lo, 1.0 / dens[0], 1.0 / dens[1])

    kpad[:, 0:ATT_BLOCK, :] = kpad[:, ts:ts + ATT_BLOCK, :]
    vpad[:, 0:ATT_BLOCK, :] = vpad[:, ts:ts + ATT_BLOCK, :]

    gate = _dot(h_scr[...], wg_ref[...])
    og = (o_scr[...] * _silu(gate)).astype(bf16)
    out = _dot(og, wout_ref[...])
    o_ref[0] = x + out * _rms_scale(out) * postw_ref[...]


def _swa_layer(x, cos_t, sin_t, prew, postw, w_in, sinks, w_out):
    bsz, seq, d = x.shape
    ts = SWA_TILE
    wq = w_in[:, :ATT_WIDTH].astype(bf16)
    wk = w_in[:, ATT_WIDTH:ATT_WIDTH + ATT_KV_WIDTH].astype(bf16)
    wv = w_in[:, ATT_WIDTH + ATT_KV_WIDTH:ATT_WIDTH + 2 * ATT_KV_WIDTH].astype(bf16)
    wg = w_in[:, ATT_WIDTH + 2 * ATT_KV_WIDTH:].astype(bf16)
    consts = (prew.reshape(1, d), postw.reshape(1, d), wq, wk, wv, wg, w_out.astype(bf16))
    x_spec = pl.BlockSpec((1, ts, d), lambda b, t: (b, t, 0))
    tab_spec = pl.BlockSpec((1, ts, LANES), lambda b, t: (b, t, 0))
    in_specs = ([pl.BlockSpec(memory_space=pltpu.SMEM), x_spec, tab_spec, tab_spec]
                + [_const_spec(a.shape) for a in consts])
    return pl.pallas_call(
        _swa_kernel,
        grid=(bsz, seq // ts),
        in_specs=in_specs,
        out_specs=x_spec,
        out_shape=jax.ShapeDtypeStruct(x.shape, f32),
        scratch_shapes=[
            pltpu.VMEM((2 * ATT_KV_HEADS, ts + ATT_BLOCK, LANES), bf16),
            pltpu.VMEM((2 * ATT_KV_HEADS, ts + ATT_BLOCK, LANES), bf16),
            pltpu.VMEM((ts, ATT_WIDTH), bf16),
            pltpu.VMEM((ts, ATT_WIDTH), f32),
            pltpu.VMEM((ts, d), bf16),
        ],
        compiler_params=pltpu.CompilerParams(
            dimension_semantics=("arbitrary", "arbitrary"),
            vmem_limit_bytes=VMEM_LIMIT_BYTES),
        name="swa_layer",
    )(sinks, x, cos_t, sin_t, *consts)


def _rope_tables(positions):
    half = ROPE_DIM // 2
    inv = ROPE_THETA ** (-jnp.arange(0, ROPE_DIM, 2, dtype=f32) / ROPE_DIM)
    ang = positions.astype(f32)[..., None] * inv
    cos, sin = jnp.cos(ang), jnp.sin(ang)
    rest = ATT_HEAD_DIM - ROPE_DIM
    ones = jnp.ones(cos.shape[:-1] + (rest,), f32)
    zeros = jnp.zeros(cos.shape[:-1] + (rest,), f32)
    cos64 = jnp.concatenate([cos, cos, ones], axis=-1)
    sin64 = jnp.concatenate([-sin, sin, zeros], axis=-1)
    del half
    return jnp.concatenate([cos64, cos64], axis=-1), jnp.concatenate([sin64, sin64], axis=-1)


def kernel(x, positions, pre_norm, post_norm, ssm_w_in, ssm_conv_w, ssm_conv_b, ssm_dt_bias, ssm_a_log, ssm_d, ssm_gate_norm, ssm_w_out, att_w_in, att_sinks, att_w_out):
    cos_t, sin_t = _rope_tables(positions)
    depth = pre_norm.shape[0]
    for i in range(depth):
        j = i // 2
        if i % 2 == 0:
            x = _ssd_layer(x, pre_norm[i], post_norm[i], ssm_w_in[j], ssm_conv_w[j], ssm_conv_b[j],
                           ssm_dt_bias[j], ssm_a_log[j], ssm_d[j], ssm_gate_norm[j], ssm_w_out[j])
        else:
            x = _swa_layer(x, cos_t, sin_t, pre_norm[i], post_norm[i], att_w_in[j], att_sinks[j],
                           att_w_out[j])
    return x
```

```python
import functools

import jax
import jax.numpy as jnp
from jax import lax
from jax.experimental import pallas as pl
from jax.experimental.pallas import tpu as pltpu

f32 = jnp.float32
bf16 = jnp.bfloat16

EPS = 1e-6
LANES = 128
SUBLANES = 8
VMEM_LIMIT_BYTES = 56 * 1024 * 1024

D_MODEL = 1024
SSM_D_INNER = 2048
SSM_HEAD_DIM = 64
SSM_HEADS = 32
SSM_GROUPS = 8
SSM_STATE = 128
SSM_CONV = 4
SSM_CHUNK = 128
SSM_BC_DIM = SSM_GROUPS * SSM_STATE
SSM_CONV_DIM = SSM_D_INNER + 2 * SSM_BC_DIM
ATT_HEAD_DIM = 64
ATT_Q_HEADS = 16
ATT_KV_HEADS = 4
ATT_WIDTH = 1024
ATT_KV_WIDTH = 256
ATT_BLOCK = 128
ROPE_THETA = 500000.0
ROPE_DIM = 16

SSD_TILE = 256
SWA_TILE = 256

NT_DIMS = (((1,), (1,)), ((), ()))
TN_DIMS = (((0,), (0,)), ((), ()))


def _dot(a, b):
    return jnp.dot(a, b, preferred_element_type=f32)


def _rms_scale(v):
    return lax.rsqrt(jnp.sum(v * v, axis=-1, keepdims=True) * (1.0 / v.shape[-1]) + EPS)


def _silu(v):
    return v * jax.nn.sigmoid(v)


def _ssd_kernel(x_ref, prew_ref, postw_ref, wz_ref, wxbc_ref, wdt_ref, cw_ref, cb_ref,
                dtb_ref, alog_ref, dsk_ref, gn_ref, wout_ref, o_ref,
                cbuf, xs_scr, bc_scr, z_scr, y_scr, out_scr, h_scr, state_ref):
    ts = SSD_TILE
    nch = ts // SSM_CHUNK
    gw = 2 * LANES
    t = pl.program_id(1)

    @pl.when(t == 0)
    def _():
        cbuf[:, 0:SUBLANES, :] = jnp.zeros((cbuf.shape[0], SUBLANES, LANES), f32)
        state_ref[...] = jnp.zeros(state_ref.shape, f32)

    x = x_ref[0]
    h_scr[...] = (x * _rms_scale(x) * prew_ref[...]).astype(bf16)

    def project(g):
        for part in range(2):
            raw = _dot(h_scr[...], wxbc_ref[:, 2 * gw * g + gw * part:2 * gw * g + gw * (part + 1)])
            for half in range(2):
                cbuf[4 * g + 2 * part + half, SUBLANES:SUBLANES + ts, :] = (
                    raw[:, LANES * half:LANES * (half + 1)])
        z_scr[g] = _dot(h_scr[...], wz_ref[:, gw * g:gw * (g + 1)])

    def conv(g):
        for part in range(2):
            for half in range(2):
                i = 4 * g + 2 * part + half
                cols = slice(LANES * i, LANES * (i + 1))
                acc = cb_ref[:, cols] + cbuf[i, pl.ds(SUBLANES - 3, ts), :] * cw_ref[0:1, cols]
                for k in range(1, SSM_CONV):
                    acc = acc + cbuf[i, pl.ds(SUBLANES - 3 + k, ts), :] * cw_ref[k:k + 1, cols]
                act = _silu(acc)
                cbuf[i, 0:SUBLANES, :] = cbuf[i, ts:ts + SUBLANES, :]
                dst = slice(gw * g + LANES * half, gw * g + LANES * (half + 1))
                if part == 0:
                    xs_scr[:, dst] = act
                else:
                    bc_scr[:, dst] = act.astype(bf16)

    def out_project(g):
        cols = slice(gw * g, gw * (g + 1))
        part = _dot((y_scr[:, cols] * gn_ref[:, cols]).astype(bf16), wout_ref[cols, :])
        if g == 0:
            out_scr[...] = part
        else:
            out_scr[...] += part

    project(0)
    conv(0)
    project(1)

    dt = jax.nn.softplus(_dot(h_scr[...], wdt_ref[...]) + dtb_ref[...])
    a = dt * (-jnp.exp(alog_ref[...]))
    r_i = lax.broadcasted_iota(jnp.int32, (SSM_CHUNK, SSM_CHUNK), 0)
    c_i = lax.broadcasted_iota(jnp.int32, (SSM_CHUNK, SSM_CHUNK), 1)
    causal = r_i >= c_i
    ltri = causal.astype(f32)
    lo = c_i < SSM_HEAD_DIM
    acs_l, acst_l, dtt_l, eacs_l, w_l, cd_l = [], [], [], [], [], []
    for c in range(nch):
        rows = slice(SSM_CHUNK * c, SSM_CHUNK * (c + 1))
        acs = jnp.dot(ltri, a[rows], precision=lax.Precision.HIGHEST, preferred_element_type=f32)
        last = acs[SSM_CHUNK - 1:SSM_CHUNK, :]
        acs_l.append(acs)
        acst_l.append(acs.T)
        dtt_l.append(dt[rows].T)
        eacs_l.append(jnp.exp(acs))
        w_l.append(dt[rows] * jnp.exp(last - acs))
        cd_l.append(jnp.exp(last))

    lane256 = lax.broadcasted_iota(jnp.int32, (1, 2 * LANES), 1) // SSM_HEAD_DIM
    ssq = [jnp.zeros((SSM_CHUNK, LANES), f32) for _ in range(nch)]
    for g in range(SSM_GROUPS):
        s_g = state_ref[g]
        cbms = []
        for c in range(nch):
            rows = slice(SSM_CHUNK * c, SSM_CHUNK * (c + 1))
            cbms.append(lax.dot_general(bc_scr[rows, gw * g + LANES:gw * (g + 1)],
                                        bc_scr[rows, gw * g:gw * g + LANES],
                                        NT_DIMS, preferred_element_type=f32))
        if g + 1 < SSM_GROUPS:
            conv(g + 1)
        if g + 2 < SSM_GROUPS:
            project(g + 2)
        for c in range(nch):
            rows = slice(SSM_CHUNK * c, SSM_CHUNK * (c + 1))
            b_g = bc_scr[rows, gw * g:gw * g + LANES]
            c_g = bc_scr[rows, gw * g + LANES:gw * (g + 1)]
            cbm = cbms[c]
            if c == nch - 1 and g > 0:
                out_project(g - 1)
            yoff = _dot(c_g, s_g.astype(bf16))
            xd_parts = []
            for q in range(2):
                h0 = 4 * g + 2 * q
                cols = slice(256 * g + LANES * q, 256 * g + LANES * (q + 1))
                xp = xs_scr[rows, cols]
                xpb = xp.astype(bf16)
                ms = []
                for hh in (h0, h0 + 1):
                    seg = acs_l[c][:, hh:hh + 1] - acst_l[c][hh:hh + 1, :]
                    dec = jnp.exp(jnp.where(causal, seg, -jnp.inf))
                    ms.append((cbm * dec * dtt_l[c][hh:hh + 1, :]).astype(bf16))
                zb = jnp.zeros_like(xpb)
                bd = jnp.concatenate([jnp.where(lo, xpb, zb), jnp.where(lo, zb, xpb)], axis=0)
                ydiag = _dot(jnp.concatenate(ms, axis=1), bd)
                e_pair = jnp.where(lo, eacs_l[c][:, h0:h0 + 1], eacs_l[c][:, h0 + 1:h0 + 2])
                w_pair = jnp.where(lo, w_l[c][:, h0:h0 + 1], w_l[c][:, h0 + 1:h0 + 2])
                y = ydiag + yoff[:, LANES * q:LANES * (q + 1)] * e_pair + dsk_ref[:, cols] * xp
                yg = y * _silu(z_scr[g, rows, LANES * q:LANES * (q + 1)])
                y_scr[rows, cols] = yg
                ssq[c] = ssq[c] + yg * yg
                xd_parts.append((xp * w_pair).astype(bf16))
            xd = jnp.concatenate(xd_parts, axis=1)
            cd_row = jnp.zeros((1, 2 * LANES), f32)
            for r in range(4):
                cd_row = jnp.where(lane256 == r, cd_l[c][:, 4 * g + r:4 * g + r + 1], cd_row)
            s_g = s_g * cd_row + lax.dot_general(b_g, xd, TN_DIMS, preferred_element_type=f32)
        state_ref[g] = s_g

    out_project(SSM_GROUPS - 1)
    ssq_all = jnp.concatenate(ssq, axis=0)
    rms = lax.rsqrt(jnp.sum(ssq_all, axis=-1, keepdims=True) * (1.0 / SSM_D_INNER) + EPS)
    out = out_scr[...] * rms
    o_ref[0] = x + out * _rms_scale(out) * postw_ref[...]


def _const_spec(shape):
    nd = len(shape)
    return pl.BlockSpec(shape, lambda b, t: (0,) * nd, pipeline_mode=pl.Buffered(1))


def _ssd_layer(x, prew, postw, w_in, conv_w, conv_b, dt_bias, a_log, d_skip, gate_norm, w_out):
    bsz, seq, d = x.shape
    ts = SSD_TILE
    def group_major(a):
        lead = a.shape[:-1]
        xs = a[..., :SSM_D_INNER].reshape(lead + (SSM_GROUPS, -1))
        bm = a[..., SSM_D_INNER:SSM_D_INNER + SSM_BC_DIM].reshape(lead + (SSM_GROUPS, -1))
        cm = a[..., SSM_D_INNER + SSM_BC_DIM:].reshape(lead + (SSM_GROUPS, -1))
        return jnp.concatenate([xs, bm, cm], axis=-1).reshape(lead + (SSM_CONV_DIM,))

    wz = w_in[:, :SSM_D_INNER].astype(bf16)
    wxbc = group_major(w_in[:, SSM_D_INNER:SSM_D_INNER + SSM_CONV_DIM]).astype(bf16)
    wdt = w_in[:, SSM_D_INNER + SSM_CONV_DIM:].astype(bf16)
    args = (x, prew.reshape(1, d), postw.reshape(1, d), wz, wxbc, wdt,
            group_major(conv_w), group_major(conv_b.reshape(1, -1)),
            dt_bias.reshape(1, -1), a_log.reshape(1, -1),
            jnp.repeat(d_skip, SSM_HEAD_DIM).reshape(1, -1), gate_norm.reshape(1, -1),
            w_out.astype(bf16))
    x_spec = pl.BlockSpec((1, ts, d), lambda b, t: (b, t, 0))
    in_specs = [x_spec] + [_const_spec(a.shape) for a in args[1:]]
    return pl.pallas_call(
        _ssd_kernel,
        grid=(bsz, seq // ts),
        in_specs=in_specs,
        out_specs=x_spec,
        out_shape=jax.ShapeDtypeStruct(x.shape, f32),
        scratch_shapes=[
            pltpu.VMEM((SSM_CONV_DIM // LANES, ts + SUBLANES, LANES), f32),
            pltpu.VMEM((ts, SSM_D_INNER), f32),
            pltpu.VMEM((ts, 2 * SSM_BC_DIM), bf16),
            pltpu.VMEM((SSM_GROUPS, ts, 2 * LANES), f32),
            pltpu.VMEM((ts, SSM_D_INNER), f32),
            pltpu.VMEM((ts, d), f32),
            pltpu.VMEM((ts, d), bf16),
            pltpu.VMEM((SSM_GROUPS, SSM_STATE, 4 * SSM_HEAD_DIM), f32),
        ],
        compiler_params=pltpu.CompilerParams(
            dimension_semantics=("arbitrary", "arbitrary"),
            vmem_limit_bytes=VMEM_LIMIT_BYTES),
        name="ssd_layer",
    )(*args)


def _swa_kernel(sink_ref, x_ref, cos_ref, sin_ref, prew_ref, postw_ref, wq_ref, wk_ref, wv_ref,
                wg_ref, wout_ref, o_ref, kpad, vpad, q_scr, o_scr, h_scr, s_scr, g_scr):
    ts = SWA_TILE
    nqb = ts // ATT_BLOCK
    t = pl.program_id(1)

    @pl.when(t == 0)
    def _():
        kpad[:, 0:ATT_BLOCK, :] = jnp.zeros((kpad.shape[0], ATT_BLOCK, LANES), bf16)
        vpad[:, 0:ATT_BLOCK, :] = jnp.zeros((vpad.shape[0], ATT_BLOCK, LANES), bf16)

    x = x_ref[0]
    h_scr[...] = (x * _rms_scale(x) * prew_ref[...]).astype(bf16)

    cos_t = cos_ref[0]
    sin_t = sin_ref[0]
    l64 = lax.broadcasted_iota(jnp.int32, (ts, LANES), 1) % ATT_HEAD_DIM
    hi_half = lax.broadcasted_iota(jnp.int32, (ts, LANES), 1) >= ATT_HEAD_DIM

    def rope(tb, c, s):
        sw = jnp.where(l64 < ROPE_DIM // 2, pltpu.roll(tb, LANES - ROPE_DIM // 2, axis=1),
                       jnp.where(l64 < ROPE_DIM, pltpu.roll(tb, ROPE_DIM // 2, axis=1), 0.0))
        return tb * c + sw * s

    scale = ATT_HEAD_DIM ** -0.5
    cos_q = cos_t * scale
    sin_q = sin_t * scale
    for j in range(ATT_WIDTH // LANES):
        qb = _dot(h_scr[...], wq_ref[:, LANES * j:LANES * (j + 1)])
        q_scr[:, LANES * j:LANES * (j + 1)] = rope(qb, cos_q, sin_q).astype(bf16)

    k = _dot(h_scr[...], wk_ref[...])
    v = _dot(h_scr[...], wv_ref[...])
    new = slice(ATT_BLOCK, ATT_BLOCK + ts)
    for kb in range(ATT_KV_WIDTH // LANES):
        kblk = rope(k[:, LANES * kb:LANES * (kb + 1)], cos_t, sin_t)
        vblk = v[:, LANES * kb:LANES * (kb + 1)]
        ksw = pltpu.roll(kblk, ATT_HEAD_DIM, axis=1)
        vsw = pltpu.roll(vblk, ATT_HEAD_DIM, axis=1)
        for hk in range(2):
            kh = 2 * kb + hk
            for hq in range(2):
                ksrc = kblk if hq == hk else ksw
                vsrc = vblk if hq == hk else vsw
                keep = hi_half if hq == 1 else jnp.logical_not(hi_half)
                kpad[2 * kh + hq, new, :] = jnp.where(keep, ksrc, 0.0).astype(bf16)
                vpad[2 * kh + hq, new, :] = jnp.where(keep, vsrc, 0.0).astype(bf16)

    qi = lax.broadcasted_iota(jnp.int32, (ATT_BLOCK, 2 * ATT_BLOCK), 0)
    kj = lax.broadcasted_iota(jnp.int32, (ATT_BLOCK, 2 * ATT_BLOCK), 1)
    dist = qi + ATT_BLOCK - kj
    in_window = jnp.logical_and(dist >= 0, dist < ATT_BLOCK)
    first_ok = jnp.logical_and(in_window, jnp.logical_or(kj >= ATT_BLOCK, t > 0))
    lo = lax.broadcasted_iota(jnp.int32, (ATT_BLOCK, LANES), 1) < ATT_HEAD_DIM
    for i in range(nqb):
        rows = slice(ATT_BLOCK * i, ATT_BLOCK * (i + 1))
        keys = slice(ATT_BLOCK * i, ATT_BLOCK * (i + 2))
        for j in range(ATT_WIDTH // LANES):
            kh = j // 2
            qblk = q_scr[rows, LANES * j:LANES * (j + 1)]
            for hq in range(2):
                s_scr[ATT_Q_HEADS * i + 2 * j + hq] = lax.dot_general(
                    qblk, kpad[2 * kh + hq, keys, :], NT_DIMS, preferred_element_type=f32)
    g_scr[...] = _silu(_dot(h_scr[...], wg_ref[...]))

    for i in range(nqb):
        rows = slice(ATT_BLOCK * i, ATT_BLOCK * (i + 1))
        keys = slice(ATT_BLOCK * i, ATT_BLOCK * (i + 2))
        valid = first_ok if i == 0 else in_window
        for j in range(ATT_WIDTH // LANES):
            kh = j // 2
            ps, dens = [], []
            for hq in range(2):
                sink = sink_ref[2 * j + hq]
                s = jnp.where(valid, s_scr[ATT_Q_HEADS * i + 2 * j + hq], -jnp.inf)
                m = jnp.maximum(jnp.max(s, axis=-1, keepdims=True), sink)
                p = jnp.exp(s - m)
                dens.append(jnp.sum(p, axis=-1, keepdims=True) + jnp.exp(sink - m))
                ps.append(p.astype(bf16))
            vcat = jnp.concatenate([vpad[2 * kh, keys, :], vpad[2 * kh + 1, keys, :]], axis=0)
            o = _dot(jnp.concatenate(ps, axis=1), vcat)
            o_scr[rows, LANES * j:LANES * (j + 1)] = o * jnp.where(lo, 1.0 / dens[0], 1.0 / dens[1])

    kpad[:, 0:ATT_BLOCK, :] = kpad[:, ts:ts + ATT_BLOCK, :]
    vpad[:, 0:ATT_BLOCK, :] = vpad[:, ts:ts + ATT_BLOCK, :]

    og = (o_scr[...] * g_scr[...]).astype(bf16)
    out = _dot(og, wout_ref[...])
    o_ref[0] = x + out * _rms_scale(out) * postw_ref[...]


def _swa_layer(x, cos_t, sin_t, prew, postw, w_in, sinks, w_out):
    bsz, seq, d = x.shape
    ts = SWA_TILE
    wq = w_in[:, :ATT_WIDTH].astype(bf16)
    wk = w_in[:, ATT_WIDTH:ATT_WIDTH + ATT_KV_WIDTH].astype(bf16)
    wv = w_in[:, ATT_WIDTH + ATT_KV_WIDTH:ATT_WIDTH + 2 * ATT_KV_WIDTH].astype(bf16)
    wg = w_in[:, ATT_WIDTH + 2 * ATT_KV_WIDTH:].astype(bf16)
    consts = (prew.reshape(1, d), postw.reshape(1, d), wq, wk, wv, wg, w_out.astype(bf16))
    x_spec = pl.BlockSpec((1, ts, d), lambda b, t: (b, t, 0))
    tab_spec = pl.BlockSpec((1, ts, LANES), lambda b, t: (b, t, 0))
    in_specs = ([pl.BlockSpec(memory_space=pltpu.SMEM), x_spec, tab_spec, tab_spec]
                + [_const_spec(a.shape) for a in consts])
    return pl.pallas_call(
        _swa_kernel,
        grid=(bsz, seq // ts),
        in_specs=in_specs,
        out_specs=x_spec,
        out_shape=jax.ShapeDtypeStruct(x.shape, f32),
        scratch_shapes=[
            pltpu.VMEM((2 * ATT_KV_HEADS, ts + ATT_BLOCK, LANES), bf16),
            pltpu.VMEM((2 * ATT_KV_HEADS, ts + ATT_BLOCK, LANES), bf16),
            pltpu.VMEM((ts, ATT_WIDTH), bf16),
            pltpu.VMEM((ts, ATT_WIDTH), f32),
            pltpu.VMEM((ts, d), bf16),
            pltpu.VMEM((ATT_Q_HEADS * ts // ATT_BLOCK, ATT_BLOCK, 2 * ATT_BLOCK), f32),
            pltpu.VMEM((ts, ATT_WIDTH), f32),
        ],
        compiler_params=pltpu.CompilerParams(
            dimension_semantics=("arbitrary", "arbitrary"),
            vmem_limit_bytes=VMEM_LIMIT_BYTES),
        name="swa_layer",
    )(sinks, x, cos_t, sin_t, *consts)


def _rope_tables(positions):
    half = ROPE_DIM // 2
    inv = ROPE_THETA ** (-jnp.arange(0, ROPE_DIM, 2, dtype=f32) / ROPE_DIM)
    ang = positions.astype(f32)[..., None] * inv
    cos, sin = jnp.cos(ang), jnp.sin(ang)
    rest = ATT_HEAD_DIM - ROPE_DIM
    ones = jnp.ones(cos.shape[:-1] + (rest,), f32)
    zeros = jnp.zeros(cos.shape[:-1] + (rest,), f32)
    cos64 = jnp.concatenate([cos, cos, ones], axis=-1)
    sin64 = jnp.concatenate([-sin, sin, zeros], axis=-1)
    del half
    return jnp.concatenate([cos64, cos64], axis=-1), jnp.concatenate([sin64, sin64], axis=-1)


def kernel(x, positions, pre_norm, post_norm, ssm_w_in, ssm_conv_w, ssm_conv_b, ssm_dt_bias, ssm_a_log, ssm_d, ssm_gate_norm, ssm_w_out, att_w_in, att_sinks, att_w_out):
    cos_t, sin_t = _rope_tables(positions)
    depth = pre_norm.shape[0]
    for i in range(depth):
        j = i // 2
        if i % 2 == 0:
            x = _ssd_layer(x, pre_norm[i], post_norm[i], ssm_w_in[j], ssm_conv_w[j], ssm_conv_b[j],
                           ssm_dt_bias[j], ssm_a_log[j], ssm_d[j], ssm_gate_norm[j], ssm_w_out[j])
        else:
            x = _swa_layer(x, cos_t, sin_t, pre_norm[i], post_norm[i], att_w_in[j], att_sinks[j],
                           att_w_out[j])
    return x
```

```python
import jax
import jax.numpy as jnp
import numpy as np
from jax import lax
from jax.experimental import pallas as pl
from jax.experimental.pallas import tpu as pltpu

f32 = jnp.float32
bf16 = jnp.bfloat16

EPS = 1e-6
LOG2E = 1.4426950408889634
LANES = 128
SUBLANES = 8
VMEM_LIMIT_BYTES = 56 * 1024 * 1024

D_MODEL = 1024
SSM_D_INNER = 2048
SSM_HEAD_DIM = 64
SSM_HEADS = 32
SSM_GROUPS = 8
SSM_STATE = 128
SSM_CONV = 4
SSM_CHUNK = 128
SSM_BC_DIM = SSM_GROUPS * SSM_STATE
SSM_CONV_DIM = SSM_D_INNER + 2 * SSM_BC_DIM
ATT_HEAD_DIM = 64
ATT_Q_HEADS = 16
ATT_KV_HEADS = 4
ATT_WIDTH = 1024
ATT_KV_WIDTH = 256
ATT_BLOCK = 128
ROPE_THETA = 500000.0
ROPE_DIM = 16

SSD_TILE = 256
SWA_TILE = 256

NT_DIMS = (((1,), (1,)), ((), ()))
TN_DIMS = (((0,), (0,)), ((), ()))


def _dot(a, b):
    return jnp.dot(a, b, preferred_element_type=f32)


def _rms_scale(v):
    return lax.rsqrt(jnp.sum(v * v, axis=-1, keepdims=True) * (1.0 / v.shape[-1]) + EPS)


def _silu(v):
    return v * jax.nn.sigmoid(v)


def _col_blocks(w):
    return w.reshape(w.shape[0], -1, 2 * LANES).transpose(1, 0, 2)


def _ssd_kernel(x_ref, prew_ref, postw_ref, wz_ref, wxbc_ref, wdt_ref, cw_ref, cb_ref,
                dtb_ref, alog_ref, dsk_ref, gn_ref, wout_ref, o_ref,
                cbuf, xs_scr, bc_scr, z_scr, y_scr, out_scr, h_scr, state_ref):
    ts = SSD_TILE
    nch = ts // SSM_CHUNK
    gw = 2 * LANES
    t = pl.program_id(1)

    @pl.when(t == 0)
    def _():
        cbuf[:, 0:SUBLANES, :] = jnp.zeros((cbuf.shape[0], SUBLANES, LANES), f32)
        state_ref[...] = jnp.zeros(state_ref.shape, f32)

    x = x_ref[0]
    h_scr[...] = (x * _rms_scale(x) * prew_ref[...]).astype(bf16)

    def project(g):
        for part in range(2):
            raw = _dot(h_scr[...], wxbc_ref[2 * g + part])
            for half in range(2):
                cbuf[4 * g + 2 * part + half, SUBLANES:SUBLANES + ts, :] = (
                    raw[:, LANES * half:LANES * (half + 1)])
        z_scr[g] = _dot(h_scr[...], wz_ref[g])

    def conv(g):
        for part in range(2):
            for half in range(2):
                i = 4 * g + 2 * part + half
                cols = slice(LANES * i, LANES * (i + 1))
                acc = cb_ref[:, cols] + cbuf[i, pl.ds(SUBLANES - 3, ts), :] * cw_ref[0:1, cols]
                for k in range(1, SSM_CONV):
                    acc = acc + cbuf[i, pl.ds(SUBLANES - 3 + k, ts), :] * cw_ref[k:k + 1, cols]
                act = _silu(acc)
                cbuf[i, 0:SUBLANES, :] = cbuf[i, ts:ts + SUBLANES, :]
                dst = slice(gw * g + LANES * half, gw * g + LANES * (half + 1))
                if part == 0:
                    xs_scr[:, dst] = act
                else:
                    bc_scr[:, dst] = act.astype(bf16)

    def out_project(g):
        cols = slice(gw * g, gw * (g + 1))
        part = _dot((y_scr[:, cols] * gn_ref[:, cols]).astype(bf16), wout_ref[cols, :])
        if g == 0:
            out_scr[...] = part
        else:
            out_scr[...] += part

    project(0)
    conv(0)
    project(1)

    dt = jax.nn.softplus(_dot(h_scr[...], wdt_ref[...]) + dtb_ref[...])
    a = dt * (-jnp.exp(alog_ref[...]))
    r_i = lax.broadcasted_iota(jnp.int32, (SSM_CHUNK, SSM_CHUNK), 0)
    c_i = lax.broadcasted_iota(jnp.int32, (SSM_CHUNK, SSM_CHUNK), 1)
    causal = r_i >= c_i
    ltri = causal.astype(f32)
    lo = c_i < SSM_HEAD_DIM
    acs2_l, rowt_l, eacs_l, w_l, cd_l = [], [], [], [], []
    for c in range(nch):
        rows = slice(SSM_CHUNK * c, SSM_CHUNK * (c + 1))
        acs = jnp.dot(ltri, a[rows], precision=lax.Precision.HIGHEST, preferred_element_type=f32)
        last = acs[SSM_CHUNK - 1:SSM_CHUNK, :]
        acs2 = acs * LOG2E
        acs2_l.append(acs2)
        rowt_l.append((acs2 - jnp.log(dt[rows]) * LOG2E).T)
        eacs_l.append(jnp.exp(acs))
        w_l.append(dt[rows] * jnp.exp(last - acs))
        cd_l.append(jnp.exp(last))

    lane256 = lax.broadcasted_iota(jnp.int32, (1, 2 * LANES), 1) // SSM_HEAD_DIM
    ssq = [jnp.zeros((SSM_CHUNK, LANES), f32) for _ in range(nch)]
    for g in range(SSM_GROUPS):
        s_g = state_ref[g]
        cbms = []
        for c in range(nch):
            rows = slice(SSM_CHUNK * c, SSM_CHUNK * (c + 1))
            cbms.append(lax.dot_general(bc_scr[rows, gw * g + LANES:gw * (g + 1)],
                                        bc_scr[rows, gw * g:gw * g + LANES],
                                        NT_DIMS, preferred_element_type=f32))
        if g + 1 < SSM_GROUPS:
            conv(g + 1)
        if g + 2 < SSM_GROUPS:
            project(g + 2)
        for c in range(nch):
            rows = slice(SSM_CHUNK * c, SSM_CHUNK * (c + 1))
            b_g = bc_scr[rows, gw * g:gw * g + LANES]
            c_g = bc_scr[rows, gw * g + LANES:gw * (g + 1)]
            cbm = cbms[c]
            if c == nch - 1 and g > 0:
                out_project(g - 1)
            yoff = _dot(c_g, s_g.astype(bf16))
            xd_parts = []
            for q in range(2):
                h0 = 4 * g + 2 * q
                cols = slice(256 * g + LANES * q, 256 * g + LANES * (q + 1))
                xp = xs_scr[rows, cols]
                xpb = xp.astype(bf16)
                ms = []
                for hh in (h0, h0 + 1):
                    seg2 = acs2_l[c][:, hh:hh + 1] - rowt_l[c][hh:hh + 1, :]
                    ms.append((cbm * jnp.exp2(jnp.where(causal, seg2, -jnp.inf))).astype(bf16))
                zb = jnp.zeros_like(xpb)
                bd = jnp.concatenate([jnp.where(lo, xpb, zb), jnp.where(lo, zb, xpb)], axis=0)
                ydiag = _dot(jnp.concatenate(ms, axis=1), bd)
                e_pair = jnp.where(lo, eacs_l[c][:, h0:h0 + 1], eacs_l[c][:, h0 + 1:h0 + 2])
                w_pair = jnp.where(lo, w_l[c][:, h0:h0 + 1], w_l[c][:, h0 + 1:h0 + 2])
                y = ydiag + yoff[:, LANES * q:LANES * (q + 1)] * e_pair + dsk_ref[:, cols] * xp
                yg = y * _silu(z_scr[g, rows, LANES * q:LANES * (q + 1)])
                y_scr[rows, cols] = yg
                ssq[c] = ssq[c] + yg * yg
                xd_parts.append((xp * w_pair).astype(bf16))
            xd = jnp.concatenate(xd_parts, axis=1)
            cd_row = jnp.zeros((1, 2 * LANES), f32)
            for r in range(4):
                cd_row = jnp.where(lane256 == r, cd_l[c][:, 4 * g + r:4 * g + r + 1], cd_row)
            s_g = s_g * cd_row + lax.dot_general(b_g, xd, TN_DIMS, preferred_element_type=f32)
        state_ref[g] = s_g

    out_project(SSM_GROUPS - 1)
    ssq_all = jnp.concatenate(ssq, axis=0)
    rms = lax.rsqrt(jnp.sum(ssq_all, axis=-1, keepdims=True) * (1.0 / SSM_D_INNER) + EPS)
    out = out_scr[...] * rms
    o_ref[0] = x + out * _rms_scale(out) * postw_ref[...]


def _const_spec(shape):
    nd = len(shape)
    return pl.BlockSpec(shape, lambda b, t: (0,) * nd, pipeline_mode=pl.Buffered(1))


def _ssd_layer(x, prew, postw, w_in, conv_w, conv_b, dt_bias, a_log, d_skip, gate_norm, w_out):
    bsz, seq, d = x.shape
    ts = SSD_TILE

    def group_major(a):
        lead = a.shape[:-1]
        xs = a[..., :SSM_D_INNER].reshape(lead + (SSM_GROUPS, -1))
        bm = a[..., SSM_D_INNER:SSM_D_INNER + SSM_BC_DIM].reshape(lead + (SSM_GROUPS, -1))
        cm = a[..., SSM_D_INNER + SSM_BC_DIM:].reshape(lead + (SSM_GROUPS, -1))
        return jnp.concatenate([xs, bm, cm], axis=-1).reshape(lead + (SSM_CONV_DIM,))

    wz = _col_blocks(w_in[:, :SSM_D_INNER].astype(bf16))
    wxbc = _col_blocks(group_major(w_in[:, SSM_D_INNER:SSM_D_INNER + SSM_CONV_DIM]).astype(bf16))
    wdt = w_in[:, SSM_D_INNER + SSM_CONV_DIM:].astype(bf16)
    args = (x, prew.reshape(1, d), postw.reshape(1, d), wz, wxbc, wdt,
            group_major(conv_w), group_major(conv_b.reshape(1, -1)),
            dt_bias.reshape(1, -1), a_log.reshape(1, -1),
            jnp.repeat(d_skip, SSM_HEAD_DIM).reshape(1, -1), gate_norm.reshape(1, -1),
            w_out.astype(bf16))
    x_spec = pl.BlockSpec((1, ts, d), lambda b, t: (b, t, 0))
    in_specs = [x_spec] + [_const_spec(a.shape) for a in args[1:]]
    return pl.pallas_call(
        _ssd_kernel,
        grid=(bsz, seq // ts),
        in_specs=in_specs,
        out_specs=x_spec,
        out_shape=jax.ShapeDtypeStruct(x.shape, f32),
        scratch_shapes=[
            pltpu.VMEM((SSM_CONV_DIM // LANES, ts + SUBLANES, LANES), f32),
            pltpu.VMEM((ts, SSM_D_INNER), f32),
            pltpu.VMEM((ts, 2 * SSM_BC_DIM), bf16),
            pltpu.VMEM((SSM_GROUPS, ts, 2 * LANES), f32),
            pltpu.VMEM((ts, SSM_D_INNER), f32),
            pltpu.VMEM((ts, d), f32),
            pltpu.VMEM((ts, d), bf16),
            pltpu.VMEM((SSM_GROUPS, SSM_STATE, 4 * SSM_HEAD_DIM), f32),
        ],
        compiler_params=pltpu.CompilerParams(
            dimension_semantics=("arbitrary", "arbitrary"),
            vmem_limit_bytes=VMEM_LIMIT_BYTES),
        name="ssd_layer",
    )(*args)


def _swa_kernel(sink_ref, x_ref, pos_ref, inv_ref, sel_ref, prew_ref, postw_ref, wq_ref, wk_ref,
                wv_ref, wg_ref, wout_ref, o_ref, kpad, vpad, q_scr, o_scr, h_scr, s_scr, g_scr):
    ts = SWA_TILE
    nqb = ts // ATT_BLOCK
    t = pl.program_id(1)

    @pl.when(t == 0)
    def _():
        kpad[:, 0:ATT_BLOCK, :] = jnp.zeros((kpad.shape[0], ATT_BLOCK, LANES), bf16)
        vpad[:, 0:ATT_BLOCK, :] = jnp.zeros((vpad.shape[0], ATT_BLOCK, LANES), bf16)

    x = x_ref[0]
    h_scr[...] = (x * _rms_scale(x) * prew_ref[...]).astype(bf16)

    ang = inv_ref[...] * pos_ref[0].astype(f32)
    pieces = []
    for val in (jnp.cos(ang), jnp.sin(ang)):
        hi = val.astype(bf16).astype(f32)
        mid = (val - hi).astype(bf16).astype(f32)
        pieces += [hi, mid, (val - hi - mid).astype(bf16).astype(f32)]
    pieces.append(jnp.zeros((LANES - 8 * len(pieces), ts), f32))
    tabs = _dot(jnp.concatenate(pieces, axis=0).T.astype(bf16), sel_ref[...])
    l64 = lax.broadcasted_iota(jnp.int32, (ts, LANES), 1) % ATT_HEAD_DIM
    hi_half = lax.broadcasted_iota(jnp.int32, (ts, LANES), 1) >= ATT_HEAD_DIM
    cos_t = jnp.where(l64 < ROPE_DIM, tabs[:, :LANES], 1.0)
    sin_t = tabs[:, LANES:]

    def rope(tb, c, s):
        sw = jnp.where(l64 < ROPE_DIM // 2, pltpu.roll(tb, LANES - ROPE_DIM // 2, axis=1),
                       jnp.where(l64 < ROPE_DIM, pltpu.roll(tb, ROPE_DIM // 2, axis=1), 0.0))
        return tb * c + sw * s

    scale = ATT_HEAD_DIM ** -0.5 * LOG2E
    cos_q = cos_t * scale
    sin_q = sin_t * scale
    for jj in range(ATT_WIDTH // (2 * LANES)):
        qb = _dot(h_scr[...], wq_ref[jj])
        for half in range(2):
            cols = slice(2 * LANES * jj + LANES * half, 2 * LANES * jj + LANES * (half + 1))
            q_scr[:, cols] = rope(qb[:, LANES * half:LANES * (half + 1)], cos_q, sin_q).astype(bf16)

    k = _dot(h_scr[...], wk_ref[...])
    v = _dot(h_scr[...], wv_ref[...])
    new = slice(ATT_BLOCK, ATT_BLOCK + ts)
    for kb in range(ATT_KV_WIDTH // LANES):
        kblk = rope(k[:, LANES * kb:LANES * (kb + 1)], cos_t, sin_t)
        vblk = v[:, LANES * kb:LANES * (kb + 1)]
        ksw = pltpu.roll(kblk, ATT_HEAD_DIM, axis=1)
        vsw = pltpu.roll(vblk, ATT_HEAD_DIM, axis=1)
        for hk in range(2):
            kh = 2 * kb + hk
            for hq in range(2):
                ksrc = kblk if hq == hk else ksw
                vsrc = vblk if hq == hk else vsw
                keep = hi_half if hq == 1 else jnp.logical_not(hi_half)
                kpad[2 * kh + hq, new, :] = jnp.where(keep, ksrc, 0.0).astype(bf16)
                vpad[2 * kh + hq, new, :] = jnp.where(keep, vsrc, 0.0).astype(bf16)

    for i in range(nqb):
        rows = slice(ATT_BLOCK * i, ATT_BLOCK * (i + 1))
        keys = slice(ATT_BLOCK * i, ATT_BLOCK * (i + 2))
        for j in range(ATT_WIDTH // LANES):
            kh = j // 2
            qblk = q_scr[rows, LANES * j:LANES * (j + 1)]
            for hq in range(2):
                s_scr[ATT_Q_HEADS * i + 2 * j + hq] = lax.dot_general(
                    qblk, kpad[2 * kh + hq, keys, :], NT_DIMS, preferred_element_type=f32)
    for jj in range(ATT_WIDTH // (2 * LANES)):
        g_scr[:, 2 * LANES * jj:2 * LANES * (jj + 1)] = _silu(_dot(h_scr[...], wg_ref[jj]))

    qi = lax.broadcasted_iota(jnp.int32, (ATT_BLOCK, 2 * ATT_BLOCK), 0)
    kj = lax.broadcasted_iota(jnp.int32, (ATT_BLOCK, 2 * ATT_BLOCK), 1)
    dist = qi + ATT_BLOCK - kj
    in_window = jnp.logical_and(dist >= 0, dist < ATT_BLOCK)
    first_ok = jnp.logical_and(in_window, jnp.logical_or(kj >= ATT_BLOCK, t > 0))
    lo = lax.broadcasted_iota(jnp.int32, (ATT_BLOCK, LANES), 1) < ATT_HEAD_DIM
    for i in range(nqb):
        rows = slice(ATT_BLOCK * i, ATT_BLOCK * (i + 1))
        keys = slice(ATT_BLOCK * i, ATT_BLOCK * (i + 2))
        valid = first_ok if i == 0 else in_window
        for j in range(ATT_WIDTH // LANES):
            kh = j // 2
            ps, dens = [], []
            for hq in range(2):
                sink2 = sink_ref[2 * j + hq] * LOG2E
                s = jnp.where(valid, s_scr[ATT_Q_HEADS * i + 2 * j + hq], -jnp.inf)
                m = jnp.maximum(jnp.max(s, axis=-1, keepdims=True), sink2)
                p = jnp.exp2(s - m)
                ps.append(p.astype(bf16))
                dens.append(jnp.sum(p, axis=-1, keepdims=True) + jnp.exp2(sink2 - m))
            vcat = jnp.concatenate([vpad[2 * kh, keys, :], vpad[2 * kh + 1, keys, :]], axis=0)
            o = _dot(jnp.concatenate(ps, axis=1), vcat)
            o_scr[rows, LANES * j:LANES * (j + 1)] = o / jnp.where(lo, dens[0], dens[1])

    kpad[:, 0:ATT_BLOCK, :] = kpad[:, ts:ts + ATT_BLOCK, :]
    vpad[:, 0:ATT_BLOCK, :] = vpad[:, ts:ts + ATT_BLOCK, :]

    og = (o_scr[...] * g_scr[...]).astype(bf16)
    out = _dot(og, wout_ref[...])
    o_ref[0] = x + out * _rms_scale(out) * postw_ref[...]


def _rope_selector():
    sel = np.zeros((LANES, 2 * LANES), np.float32)
    half = ROPE_DIM // 2
    for r in range(6 * half):
        kind, f = r // (3 * half), r % half
        for lane in range(LANES):
            d = lane % ATT_HEAD_DIM
            if d < ROPE_DIM and d % half == f:
                sel[r, LANES * kind + lane] = -1.0 if (kind == 1 and d < half) else 1.0
    return jnp.asarray(sel, bf16)


def _swa_layer(x, positions, prew, postw, w_in, sinks, w_out):
    bsz, seq, d = x.shape
    ts = SWA_TILE
    nt = seq // ts
    wq = _col_blocks(w_in[:, :ATT_WIDTH].astype(bf16))
    wk = w_in[:, ATT_WIDTH:ATT_WIDTH + ATT_KV_WIDTH].astype(bf16)
    wv = w_in[:, ATT_WIDTH + ATT_KV_WIDTH:ATT_WIDTH + 2 * ATT_KV_WIDTH].astype(bf16)
    wg = _col_blocks(w_in[:, ATT_WIDTH + 2 * ATT_KV_WIDTH:].astype(bf16))
    inv = ROPE_THETA ** (-jnp.arange(0, ROPE_DIM, 2, dtype=f32) / ROPE_DIM)
    consts = (jnp.broadcast_to(inv[:, None], (ROPE_DIM // 2, ts)), _rope_selector(),
              prew.reshape(1, d), postw.reshape(1, d), wq, wk, wv, wg, w_out.astype(bf16))
    x_spec = pl.BlockSpec((1, ts, d), lambda b, t: (b, t, 0))
    pos_spec = pl.BlockSpec((1, 1, ts), lambda b, t: (b * nt + t, 0, 0))
    in_specs = ([pl.BlockSpec(memory_space=pltpu.SMEM), x_spec, pos_spec]
                + [_const_spec(a.shape) for a in consts])
    return pl.pallas_call(
        _swa_kernel,
        grid=(bsz, nt),
        in_specs=in_specs,
        out_specs=x_spec,
        out_shape=jax.ShapeDtypeStruct(x.shape, f32),
        scratch_shapes=[
            pltpu.VMEM((2 * ATT_KV_HEADS, ts + ATT_BLOCK, LANES), bf16),
            pltpu.VMEM((2 * ATT_KV_HEADS, ts + ATT_BLOCK, LANES), bf16),
            pltpu.VMEM((ts, ATT_WIDTH), bf16),
            pltpu.VMEM((ts, ATT_WIDTH), f32),
            pltpu.VMEM((ts, d), bf16),
            pltpu.VMEM((ATT_Q_HEADS * ts // ATT_BLOCK, ATT_BLOCK, 2 * ATT_BLOCK), f32),
            pltpu.VMEM((ts, ATT_WIDTH), f32),
        ],
        compiler_params=pltpu.CompilerParams(
            dimension_semantics=("arbitrary", "arbitrary"),
            vmem_limit_bytes=VMEM_LIMIT_BYTES),
        name="swa_layer",
    )(sinks, x, positions.reshape(bsz * nt, 1, ts), *consts)


def kernel(x, positions, pre_norm, post_norm, ssm_w_in, ssm_conv_w, ssm_conv_b, ssm_dt_bias, ssm_a_log, ssm_d, ssm_gate_norm, ssm_w_out, att_w_in, att_sinks, att_w_out):
    depth = pre_norm.shape[0]
    for i in range(depth):
        j = i // 2
        if i % 2 == 0:
            x = _ssd_layer(x, pre_norm[i], post_norm[i], ssm_w_in[j], ssm_conv_w[j], ssm_conv_b[j],
                           ssm_dt_bias[j], ssm_a_log[j], ssm_d[j], ssm_gate_norm[j], ssm_w_out[j])
        else:
            x = _swa_layer(x, positions, pre_norm[i], post_norm[i], att_w_in[j], att_sinks[j],
                           att_w_out[j])
    return x
```

```python
import jax
import jax.numpy as jnp
import numpy as np
from jax import lax
from jax.experimental import pallas as pl
from jax.experimental.pallas import tpu as pltpu

f32 = jnp.float32
bf16 = jnp.bfloat16

EPS = 1e-6
LOG2E = 1.4426950408889634
LANES = 128
SUBLANES = 8
VMEM_LIMIT_BYTES = 56 * 1024 * 1024

D_MODEL = 1024
SSM_D_INNER = 2048
SSM_HEAD_DIM = 64
SSM_HEADS = 32
SSM_GROUPS = 8
SSM_STATE = 128
SSM_CONV = 4
SSM_CHUNK = 128
SSM_BC_DIM = SSM_GROUPS * SSM_STATE
SSM_CONV_DIM = SSM_D_INNER + 2 * SSM_BC_DIM
ATT_HEAD_DIM = 64
ATT_Q_HEADS = 16
ATT_KV_HEADS = 4
ATT_WIDTH = 1024
ATT_KV_WIDTH = 256
ATT_BLOCK = 128
ROPE_THETA = 500000.0
ROPE_DIM = 16

SSD_TILE = 256
SWA_TILE = 512
GW = 2 * LANES

NT_DIMS = (((1,), (1,)), ((), ()))
TN_DIMS = (((0,), (0,)), ((), ()))


def _dot(a, b):
    return jnp.dot(a, b, preferred_element_type=f32)


def _rms_scale(v):
    return lax.rsqrt(jnp.sum(v * v, axis=-1, keepdims=True) * (1.0 / v.shape[-1]) + EPS)


def _silu(v):
    return v * jax.nn.sigmoid(v)


def _const_spec(shape):
    nd = len(shape)
    return pl.BlockSpec(shape, lambda b, t: (0,) * nd, pipeline_mode=pl.Buffered(1))


def _col_block_spec(rows, width, index):
    return pl.BlockSpec((rows, width), lambda b, t: (0, index), pipeline_mode=pl.Buffered(1))


def _ssd_kernel(*refs):
    ng = SSM_GROUPS
    x_ref, prew_ref, postw_ref = refs[:3]
    wz, wxs, wb, wc = (refs[3 + ng * i:3 + ng * (i + 1)] for i in range(4))
    (wdt_ref, cw_ref, cb_ref, dtb_ref, alog_ref, dsk_ref, gn_ref, wout_ref, o_ref,
     cbuf, xs_scr, bc_scr, z_scr, y_scr, out_scr, h_scr, state_ref) = refs[3 + 4 * ng:]
    ts = SSD_TILE
    nch = ts // SSM_CHUNK
    t = pl.program_id(1)

    @pl.when(t == 0)
    def _():
        cbuf[:, 0:SUBLANES, :] = jnp.zeros((cbuf.shape[0], SUBLANES, LANES), f32)
        state_ref[...] = jnp.zeros(state_ref.shape, f32)

    x = x_ref[0]
    h_scr[...] = (x * _rms_scale(x) * prew_ref[...]).astype(bf16)

    def conv_cols(g, part, half):
        if part == 0:
            start = GW * g + LANES * half
        else:
            start = SSM_D_INNER + SSM_BC_DIM * half + LANES * g
        return slice(start, start + LANES)

    def project(g):
        h = h_scr[...]
        raws = (_dot(h, wxs[g][...]),
                _dot(h, jnp.concatenate([wb[g][...], wc[g][...]], axis=1)))
        for part in range(2):
            for half in range(2):
                cbuf[4 * g + 2 * part + half, SUBLANES:SUBLANES + ts, :] = (
                    raws[part][:, LANES * half:LANES * (half + 1)])
        z_scr[g] = _dot(h, wz[g][...])

    def conv(g):
        for part in range(2):
            for half in range(2):
                i = 4 * g + 2 * part + half
                cols = conv_cols(g, part, half)
                acc = cb_ref[:, cols] + cbuf[i, pl.ds(SUBLANES - 3, ts), :] * cw_ref[0:1, cols]
                for k in range(1, SSM_CONV):
                    acc = acc + cbuf[i, pl.ds(SUBLANES - 3 + k, ts), :] * cw_ref[k:k + 1, cols]
                act = _silu(acc)
                cbuf[i, 0:SUBLANES, :] = cbuf[i, ts:ts + SUBLANES, :]
                dst = slice(GW * g + LANES * half, GW * g + LANES * (half + 1))
                if part == 0:
                    xs_scr[:, dst] = act
                else:
                    bc_scr[:, dst] = act.astype(bf16)

    def out_project(g):
        cols = slice(GW * g, GW * (g + 1))
        part = _dot((y_scr[:, cols] * gn_ref[:, cols]).astype(bf16), wout_ref[cols, :])
        if g == 0:
            out_scr[...] = part
        else:
            out_scr[...] += part

    project(0)
    conv(0)
    project(1)

    dt = jax.nn.softplus(_dot(h_scr[...], wdt_ref[...]) + dtb_ref[...])
    a = dt * (-jnp.exp(alog_ref[...]))
    r_i = lax.broadcasted_iota(jnp.int32, (SSM_CHUNK, SSM_CHUNK), 0)
    c_i = lax.broadcasted_iota(jnp.int32, (SSM_CHUNK, SSM_CHUNK), 1)
    causal = r_i >= c_i
    ltri = causal.astype(f32)
    lo = c_i < SSM_HEAD_DIM
    acs2_l, rowt_l, eacs_l, w_l, cd_l = [], [], [], [], []
    for c in range(nch):
        rows = slice(SSM_CHUNK * c, SSM_CHUNK * (c + 1))
        acs = jnp.dot(ltri, a[rows], precision=lax.Precision.HIGHEST, preferred_element_type=f32)
        last = acs[SSM_CHUNK - 1:SSM_CHUNK, :]
        acs2 = acs * LOG2E
        acs2_l.append(acs2)
        rowt_l.append((acs2 - jnp.log(dt[rows]) * LOG2E).T)
        eacs_l.append(jnp.exp(acs))
        w_l.append(dt[rows] * jnp.exp(last - acs))
        cd_l.append(jnp.exp(last))

    lane256 = lax.broadcasted_iota(jnp.int32, (1, GW), 1) // SSM_HEAD_DIM
    ssq = [jnp.zeros((SSM_CHUNK, LANES), f32) for _ in range(nch)]
    for g in range(ng):
        s_g = state_ref[g]
        cbms = []
        for c in range(nch):
            rows = slice(SSM_CHUNK * c, SSM_CHUNK * (c + 1))
            cbms.append(lax.dot_general(bc_scr[rows, GW * g + LANES:GW * (g + 1)],
                                        bc_scr[rows, GW * g:GW * g + LANES],
                                        NT_DIMS, preferred_element_type=f32))
        if g + 1 < ng:
            conv(g + 1)
        if g + 2 < ng:
            project(g + 2)
        for c in range(nch):
            rows = slice(SSM_CHUNK * c, SSM_CHUNK * (c + 1))
            b_g = bc_scr[rows, GW * g:GW * g + LANES]
            c_g = bc_scr[rows, GW * g + LANES:GW * (g + 1)]
            cbm = cbms[c]
            if c == nch - 1 and g > 0:
                out_project(g - 1)
            yoff = _dot(c_g, s_g.astype(bf16))
            xd_parts = []
            for q in range(2):
                h0 = 4 * g + 2 * q
                cols = slice(GW * g + LANES * q, GW * g + LANES * (q + 1))
                xp = xs_scr[rows, cols]
                xpb = xp.astype(bf16)
                ms = []
                for hh in (h0, h0 + 1):
                    seg2 = acs2_l[c][:, hh:hh + 1] - rowt_l[c][hh:hh + 1, :]
                    ms.append((cbm * jnp.exp2(jnp.where(causal, seg2, -jnp.inf))).astype(bf16))
                zb = jnp.zeros_like(xpb)
                bd = jnp.concatenate([jnp.where(lo, xpb, zb), jnp.where(lo, zb, xpb)], axis=0)
                ydiag = _dot(jnp.concatenate(ms, axis=1), bd)
                e_pair = jnp.where(lo, eacs_l[c][:, h0:h0 + 1], eacs_l[c][:, h0 + 1:h0 + 2])
                w_pair = jnp.where(lo, w_l[c][:, h0:h0 + 1], w_l[c][:, h0 + 1:h0 + 2])
                y = ydiag + yoff[:, LANES * q:LANES * (q + 1)] * e_pair + dsk_ref[:, cols] * xp
                yg = y * _silu(z_scr[g, rows, LANES * q:LANES * (q + 1)])
                y_scr[rows, cols] = yg
                ssq[c] = ssq[c] + yg * yg
                xd_parts.append((xp * w_pair).astype(bf16))
            xd = jnp.concatenate(xd_parts, axis=1)
            cd_row = jnp.zeros((1, GW), f32)
            for r in range(4):
                cd_row = jnp.where(lane256 == r, cd_l[c][:, 4 * g + r:4 * g + r + 1], cd_row)
            s_g = s_g * cd_row + lax.dot_general(b_g, xd, TN_DIMS, preferred_element_type=f32)
        state_ref[g] = s_g

    out_project(ng - 1)
    ssq_all = jnp.concatenate(ssq, axis=0)
    rms = lax.rsqrt(jnp.sum(ssq_all, axis=-1, keepdims=True) * (1.0 / SSM_D_INNER) + EPS)
    out = out_scr[...] * rms
    o_ref[0] = x + out * _rms_scale(out) * postw_ref[...]


def _ssd_layer(x, prew, postw, w_in, conv_w, conv_b, dt_bias, a_log, d_skip, gate_norm, w_out):
    bsz, seq, d = x.shape
    ts = SSD_TILE
    nt = seq // ts
    ng = SSM_GROUPS
    x_spec = pl.BlockSpec((1, ts, d), lambda b, t: (b, t, 0))
    zcol, xcol = 0, SSM_D_INNER // GW
    bcol = (2 * SSM_D_INNER) // LANES
    ccol = bcol + SSM_BC_DIM // LANES
    w_specs = ([_col_block_spec(d, GW, zcol + g) for g in range(ng)]
               + [_col_block_spec(d, GW, xcol + g) for g in range(ng)]
               + [_col_block_spec(d, LANES, bcol + g) for g in range(ng)]
               + [_col_block_spec(d, LANES, ccol + g) for g in range(ng)]
               + [_const_spec((d, SSM_HEADS))])
    w_in = w_in.astype(bf16)
    wdt = w_in[:, SSM_D_INNER + SSM_CONV_DIM:]
    small = (conv_w, conv_b.reshape(1, -1), dt_bias.reshape(1, -1), a_log.reshape(1, -1),
             jnp.repeat(d_skip, SSM_HEAD_DIM).reshape(1, -1), gate_norm.reshape(1, -1),
             w_out.astype(bf16))
    norms = (prew.reshape(1, d), postw.reshape(1, d))
    in_specs = ([x_spec] + [_const_spec(a.shape) for a in norms] + w_specs
                + [_const_spec(a.shape) for a in small])
    return pl.pallas_call(
        _ssd_kernel,
        grid=(bsz, nt),
        in_specs=in_specs,
        out_specs=x_spec,
        out_shape=jax.ShapeDtypeStruct(x.shape, f32),
        scratch_shapes=[
            pltpu.VMEM((SSM_CONV_DIM // LANES, ts + SUBLANES, LANES), f32),
            pltpu.VMEM((ts, SSM_D_INNER), f32),
            pltpu.VMEM((ts, 2 * SSM_BC_DIM), bf16),
            pltpu.VMEM((ng, ts, GW), f32),
            pltpu.VMEM((ts, SSM_D_INNER), f32),
            pltpu.VMEM((ts, d), f32),
            pltpu.VMEM((ts, d), bf16),
            pltpu.VMEM((ng, SSM_STATE, 4 * SSM_HEAD_DIM), f32),
        ],
        compiler_params=pltpu.CompilerParams(
            dimension_semantics=("arbitrary", "arbitrary"),
            vmem_limit_bytes=VMEM_LIMIT_BYTES),
        name="ssd_layer",
    )(x, *norms, *([w_in] * (4 * ng)), wdt, *small)


def _swa_kernel(*refs):
    nqw = ATT_WIDTH // GW
    sink_ref, x_ref, pos_ref, inv_ref, sel_ref, prew_ref, postw_ref = refs[:7]
    wq = refs[7:7 + nqw]
    wk_ref, wv_ref = refs[7 + nqw:9 + nqw]
    wg = refs[9 + nqw:9 + 2 * nqw]
    (wout_ref, o_ref, kpad, vtp, q_scr, o_scr, h_scr, s_scr, g_scr) = refs[9 + 2 * nqw:]
    ts = SWA_TILE
    nqb = ts // ATT_BLOCK
    t = pl.program_id(1)

    @pl.when(t == 0)
    def _():
        kpad[:, 0:ATT_BLOCK, :] = jnp.zeros((kpad.shape[0], ATT_BLOCK, LANES), bf16)
        vtp[...] = jnp.zeros(vtp.shape, bf16)

    x = x_ref[0]
    h_scr[...] = (x * _rms_scale(x) * prew_ref[...]).astype(bf16)
    h = h_scr[...]

    ang = inv_ref[...] * pos_ref[0].astype(f32)
    pieces = []
    for val in (jnp.cos(ang), jnp.sin(ang)):
        hi = val.astype(bf16).astype(f32)
        mid = (val - hi).astype(bf16).astype(f32)
        pieces += [hi, mid, (val - hi - mid).astype(bf16).astype(f32)]
    pieces.append(jnp.zeros((LANES - 8 * len(pieces), ts), f32))
    tabs = _dot(jnp.concatenate(pieces, axis=0).T.astype(bf16), sel_ref[...])
    l64 = lax.broadcasted_iota(jnp.int32, (ts, LANES), 1) % ATT_HEAD_DIM
    hi_half = lax.broadcasted_iota(jnp.int32, (ts, LANES), 1) >= ATT_HEAD_DIM
    cos_t = jnp.where(l64 < ROPE_DIM, tabs[:, :LANES], 1.0)
    sin_t = tabs[:, LANES:]

    def rope(tb, c, s):
        sw = jnp.where(l64 < ROPE_DIM // 2, pltpu.roll(tb, LANES - ROPE_DIM // 2, axis=1),
                       jnp.where(l64 < ROPE_DIM, pltpu.roll(tb, ROPE_DIM // 2, axis=1), 0.0))
        return tb * c + sw * s

    scale = ATT_HEAD_DIM ** -0.5 * LOG2E
    cos_q = cos_t * scale
    sin_q = sin_t * scale
    for jj in range(nqw):
        qb = _dot(h, wq[jj][...])
        for half in range(2):
            cols = slice(GW * jj + LANES * half, GW * jj + LANES * (half + 1))
            q_scr[:, cols] = rope(qb[:, LANES * half:LANES * (half + 1)], cos_q, sin_q).astype(bf16)

    k = _dot(h, wk_ref[...])
    v = _dot(h, wv_ref[...])
    new = slice(ATT_BLOCK, ATT_BLOCK + ts)
    for kb in range(ATT_KV_WIDTH // LANES):
        kblk = rope(k[:, LANES * kb:LANES * (kb + 1)], cos_t, sin_t)
        ksw = pltpu.roll(kblk, ATT_HEAD_DIM, axis=1)
        vt = v[:, LANES * kb:LANES * (kb + 1)].T.astype(bf16)
        for hk in range(2):
            kh = 2 * kb + hk
            for hq in range(2):
                ksrc = kblk if hq == hk else ksw
                keep = hi_half if hq == 1 else jnp.logical_not(hi_half)
                kpad[2 * kh + hq, new, :] = jnp.where(keep, ksrc, 0.0).astype(bf16)
                vtp[2 * kh + hq, ATT_HEAD_DIM * hq:ATT_HEAD_DIM * (hq + 1), new] = (
                    vt[ATT_HEAD_DIM * hk:ATT_HEAD_DIM * (hk + 1), :])

    for i in range(nqb):
        rows = slice(ATT_BLOCK * i, ATT_BLOCK * (i + 1))
        keys = slice(ATT_BLOCK * i, ATT_BLOCK * (i + 2))
        for j in range(ATT_WIDTH // LANES):
            kh = j // 2
            klhs = jnp.concatenate([kpad[2 * kh, keys, :], kpad[2 * kh + 1, keys, :]], axis=0)
            s_scr[(ATT_WIDTH // LANES) * i + j] = lax.dot_general(
                klhs, q_scr[rows, LANES * j:LANES * (j + 1)], NT_DIMS, preferred_element_type=f32)
    for jj in range(nqw):
        g_scr[:, GW * jj:GW * (jj + 1)] = _silu(_dot(h, wg[jj][...]))

    kj = lax.broadcasted_iota(jnp.int32, (2 * ATT_BLOCK, ATT_BLOCK), 0)
    qi = lax.broadcasted_iota(jnp.int32, (2 * ATT_BLOCK, ATT_BLOCK), 1)
    dist = qi + ATT_BLOCK - kj
    in_window = jnp.logical_and(dist >= 0, dist < ATT_BLOCK)
    first_ok = jnp.logical_and(in_window, jnp.logical_or(kj >= ATT_BLOCK, t > 0))
    low_rows = lax.broadcasted_iota(jnp.int32, (LANES, ATT_BLOCK), 0) < ATT_HEAD_DIM
    for i in range(nqb):
        rows = slice(ATT_BLOCK * i, ATT_BLOCK * (i + 1))
        keys = slice(ATT_BLOCK * i, ATT_BLOCK * (i + 2))
        valid = first_ok if i == 0 else in_window
        for j in range(ATT_WIDTH // LANES):
            kh = j // 2
            ps, rdens = [], []
            for hq in range(2):
                sink2 = sink_ref[2 * j + hq] * LOG2E
                s = s_scr[(ATT_WIDTH // LANES) * i + j, 2 * ATT_BLOCK * hq:2 * ATT_BLOCK * (hq + 1), :]
                s = jnp.where(valid, s, -jnp.inf)
                m = jnp.maximum(jnp.max(s, axis=0, keepdims=True), sink2)
                p = jnp.exp2(s - m)
                rdens.append(1.0 / (jnp.sum(p, axis=0, keepdims=True) + jnp.exp2(sink2 - m)))
                ps.append(p.astype(bf16))
            vcat = jnp.concatenate([vtp[2 * kh, :, keys], vtp[2 * kh + 1, :, keys]], axis=1)
            ot = _dot(vcat, jnp.concatenate(ps, axis=0))
            ot = ot * jnp.where(low_rows, rdens[0], rdens[1])
            o_scr[rows, LANES * j:LANES * (j + 1)] = ot.T

    kpad[:, 0:ATT_BLOCK, :] = kpad[:, ts:ts + ATT_BLOCK, :]
    vtp[:, :, 0:ATT_BLOCK] = vtp[:, :, ts:ts + ATT_BLOCK]

    og = (o_scr[...] * g_scr[...]).astype(bf16)
    out = _dot(og, wout_ref[...])
    o_ref[0] = x + out * _rms_scale(out) * postw_ref[...]


def _rope_selector():
    sel = np.zeros((LANES, 2 * LANES), np.float32)
    half = ROPE_DIM // 2
    for r in range(6 * half):
        kind, f = r // (3 * half), r % half
        for lane in range(LANES):
            d = lane % ATT_HEAD_DIM
            if d < ROPE_DIM and d % half == f:
                sel[r, LANES * kind + lane] = -1.0 if (kind == 1 and d < half) else 1.0
    return jnp.asarray(sel, bf16)


def _swa_layer(x, positions, prew, postw, w_in, sinks, w_out):
    bsz, seq, d = x.shape
    ts = SWA_TILE
    nt = seq // ts
    nqw = ATT_WIDTH // GW
    x_spec = pl.BlockSpec((1, ts, d), lambda b, t: (b, t, 0))
    pos_spec = pl.BlockSpec((1, 1, ts), lambda b, t: (b * nt + t, 0, 0))
    kcol = ATT_WIDTH // GW
    gcol = (ATT_WIDTH + 2 * ATT_KV_WIDTH) // GW
    w_specs = ([_col_block_spec(d, GW, jj) for jj in range(nqw)]
               + [_col_block_spec(d, GW, kcol), _col_block_spec(d, GW, kcol + 1)]
               + [_col_block_spec(d, GW, gcol + jj) for jj in range(nqw)])
    w_in = w_in.astype(bf16)
    inv = ROPE_THETA ** (-jnp.arange(0, ROPE_DIM, 2, dtype=f32) / ROPE_DIM)
    consts = (jnp.broadcast_to(inv[:, None], (ROPE_DIM // 2, ts)), _rope_selector(),
              prew.reshape(1, d), postw.reshape(1, d))
    in_specs = ([pl.BlockSpec(memory_space=pltpu.SMEM), x_spec, pos_spec]
                + [_const_spec(a.shape) for a in consts] + w_specs
                + [_const_spec(w_out.shape)])
    return pl.pallas_call(
        _swa_kernel,
        grid=(bsz, nt),
        in_specs=in_specs,
        out_specs=x_spec,
        out_shape=jax.ShapeDtypeStruct(x.shape, f32),
        scratch_shapes=[
            pltpu.VMEM((2 * ATT_KV_HEADS, ts + ATT_BLOCK, LANES), bf16),
            pltpu.VMEM((2 * ATT_KV_HEADS, LANES, ts + ATT_BLOCK), bf16),
            pltpu.VMEM((ts, ATT_WIDTH), bf16),
            pltpu.VMEM((ts, ATT_WIDTH), f32),
            pltpu.VMEM((ts, d), bf16),
            pltpu.VMEM((ATT_WIDTH // LANES * ts // ATT_BLOCK, 4 * ATT_BLOCK, ATT_BLOCK), f32),
            pltpu.VMEM((ts, ATT_WIDTH), f32),
        ],
        compiler_params=pltpu.CompilerParams(
            dimension_semantics=("arbitrary", "arbitrary"),
            vmem_limit_bytes=VMEM_LIMIT_BYTES),
        name="swa_layer",
    )(sinks, x, positions.reshape(bsz * nt, 1, ts), *consts, *([w_in] * (2 * nqw + 2)),
      w_out.astype(bf16))


def kernel(x, positions, pre_norm, post_norm, ssm_w_in, ssm_conv_w, ssm_conv_b, ssm_dt_bias, ssm_a_log, ssm_d, ssm_gate_norm, ssm_w_out, att_w_in, att_sinks, att_w_out):
    depth = pre_norm.shape[0]
    for i in range(depth):
        j = i // 2
        if i % 2 == 0:
            x = _ssd_layer(x, pre_norm[i], post_norm[i], ssm_w_in[j], ssm_conv_w[j], ssm_conv_b[j],
                           ssm_dt_bias[j], ssm_a_log[j], ssm_d[j], ssm_gate_norm[j], ssm_w_out[j])
        else:
            x = _swa_layer(x, positions, pre_norm[i], post_norm[i], att_w_in[j], att_sinks[j],
                           att_w_out[j])
    return x
```

```python
import jax
import jax.numpy as jnp
import numpy as np
from jax import lax
from jax.experimental import pallas as pl
from jax.experimental.pallas import tpu as pltpu

f32 = jnp.float32
bf16 = jnp.bfloat16

EPS = 1e-6
LOG2E = 1.4426950408889634
LANES = 128
SUBLANES = 8
VMEM_LIMIT_BYTES = 56 * 1024 * 1024

D_MODEL = 1024
SSM_D_INNER = 2048
SSM_HEAD_DIM = 64
SSM_HEADS = 32
SSM_GROUPS = 8
SSM_STATE = 128
SSM_CONV = 4
SSM_CHUNK = 128
SSM_BC_DIM = SSM_GROUPS * SSM_STATE
SSM_CONV_DIM = SSM_D_INNER + 2 * SSM_BC_DIM
ATT_HEAD_DIM = 64
ATT_Q_HEADS = 16
ATT_KV_HEADS = 4
ATT_WIDTH = 1024
ATT_KV_WIDTH = 256
ATT_BLOCK = 128
ROPE_THETA = 500000.0
ROPE_DIM = 16

SSD_TILE = 256
SWA_TILE = 512
GW = 2 * LANES

NT_DIMS = (((1,), (1,)), ((), ()))
TN_DIMS = (((0,), (0,)), ((), ()))


def _dot(a, b):
    return jnp.dot(a, b, preferred_element_type=f32)


def _rms_scale(v):
    return lax.rsqrt(jnp.sum(v * v, axis=-1, keepdims=True) * (1.0 / v.shape[-1]) + EPS)


def _silu(v):
    half = 0.5 * v
    return half * jnp.tanh(half) + half


def _const_spec(shape):
    nd = len(shape)
    return pl.BlockSpec(shape, lambda b, t: (0,) * nd, pipeline_mode=pl.Buffered(1))


def _layer_spec(arr, layer):
    rest = arr.shape[1:]
    return pl.BlockSpec((None,) + rest, lambda b, t: (layer,) + (0,) * len(rest),
                        pipeline_mode=pl.Buffered(1))


def _col_block_spec(layer, rows, width, index):
    return pl.BlockSpec((None, rows, width), lambda b, t: (layer, 0, index),
                        pipeline_mode=pl.Buffered(1))


def _split3(v):
    hi = v.astype(bf16)
    r1 = v - hi.astype(f32)
    mid = r1.astype(bf16)
    return hi, mid, (r1 - mid.astype(f32)).astype(bf16)


def _ssd_kernel(*refs):
    ng = SSM_GROUPS
    x_ref, prew_ref, postw_ref = refs[:3]
    wz, wxs, wb, wc = (refs[3 + ng * i:3 + ng * (i + 1)] for i in range(4))
    (wdt_ref, cw_ref, cb_ref, dtb_ref, alog_ref, dsk_ref, gn_ref, wout_ref, o_ref,
     cbuf, xs_scr, bc_scr, z_scr, y_scr, out_scr, h_scr, state_ref) = refs[3 + 4 * ng:]
    ts = SSD_TILE
    nch = ts // SSM_CHUNK
    t = pl.program_id(1)

    @pl.when(t == 0)
    def _():
        cbuf[:, 0:SUBLANES, :] = jnp.zeros((cbuf.shape[0], SUBLANES, LANES), f32)
        state_ref[...] = jnp.zeros(state_ref.shape, f32)

    x = x_ref[0]
    h_scr[...] = (x * _rms_scale(x) * prew_ref[...]).astype(bf16)

    def conv_cols(g, part, half):
        if part == 0:
            start = GW * g + LANES * half
        else:
            start = SSM_D_INNER + SSM_BC_DIM * half + LANES * g
        return slice(start, start + LANES)

    def project(g):
        h = h_scr[...]
        raws = (_dot(h, wxs[g][...]),
                _dot(h, jnp.concatenate([wb[g][...], wc[g][...]], axis=1)))
        for part in range(2):
            for half in range(2):
                cbuf[4 * g + 2 * part + half, SUBLANES:SUBLANES + ts, :] = (
                    raws[part][:, LANES * half:LANES * (half + 1)])
        z_scr[g] = _dot(h, wz[g][...])

    def conv(g):
        for part in range(2):
            for half in range(2):
                i = 4 * g + 2 * part + half
                cols = conv_cols(g, part, half)
                acc = cb_ref[:, cols] + cbuf[i, pl.ds(SUBLANES - 3, ts), :] * cw_ref[0:1, cols]
                for k in range(1, SSM_CONV):
                    acc = acc + cbuf[i, pl.ds(SUBLANES - 3 + k, ts), :] * cw_ref[k:k + 1, cols]
                act = _silu(acc)
                cbuf[i, 0:SUBLANES, :] = cbuf[i, ts:ts + SUBLANES, :]
                dst = slice(GW * g + LANES * half, GW * g + LANES * (half + 1))
                if part == 0:
                    xs_scr[:, dst] = act
                else:
                    bc_scr[:, dst] = act.astype(bf16)

    def out_project(g):
        cols = slice(GW * g, GW * (g + 1))
        part = _dot((y_scr[:, cols] * gn_ref[:, cols]).astype(bf16), wout_ref[cols, :])
        if g == 0:
            out_scr[...] = part
        else:
            out_scr[...] += part

    project(0)
    conv(0)
    project(1)

    dt = jax.nn.softplus(_dot(h_scr[...], wdt_ref[...]) + dtb_ref[...])
    a = dt * (-jnp.exp(alog_ref[...]))
    r_i = lax.broadcasted_iota(jnp.int32, (SSM_CHUNK, SSM_CHUNK), 0)
    c_i = lax.broadcasted_iota(jnp.int32, (SSM_CHUNK, SSM_CHUNK), 1)
    causal = r_i >= c_i
    ltri = causal.astype(bf16)
    ltri3 = jnp.concatenate([ltri, ltri, ltri], axis=1)
    lo = c_i < SSM_HEAD_DIM
    acs2_l, rowt_l, eacs_l, w_l, cd_l = [], [], [], [], []
    for c in range(nch):
        rows = slice(SSM_CHUNK * c, SSM_CHUNK * (c + 1))
        acs = _dot(ltri3, jnp.concatenate(_split3(a[rows]), axis=0))
        last = acs[SSM_CHUNK - 1:SSM_CHUNK, :]
        acs2 = acs * LOG2E
        acs2_l.append(acs2)
        rowt_l.append((acs2 - jnp.log(dt[rows]) * LOG2E).T)
        eacs_l.append(jnp.exp(acs))
        w_l.append(dt[rows] * jnp.exp(last - acs))
        cd_l.append(jnp.exp(last))

    lane256 = lax.broadcasted_iota(jnp.int32, (1, GW), 1) // SSM_HEAD_DIM
    ssq = [jnp.zeros((SSM_CHUNK, LANES), f32) for _ in range(nch)]
    for g in range(ng):
        s_g = state_ref[g]
        cbms = []
        for c in range(nch):
            rows = slice(SSM_CHUNK * c, SSM_CHUNK * (c + 1))
            cbms.append(lax.dot_general(bc_scr[rows, GW * g + LANES:GW * (g + 1)],
                                        bc_scr[rows, GW * g:GW * g + LANES],
                                        NT_DIMS, preferred_element_type=f32))
        if g + 1 < ng:
            conv(g + 1)
        if g + 2 < ng:
            project(g + 2)
        for c in range(nch):
            rows = slice(SSM_CHUNK * c, SSM_CHUNK * (c + 1))
            b_g = bc_scr[rows, GW * g:GW * g + LANES]
            c_g = bc_scr[rows, GW * g + LANES:GW * (g + 1)]
            cbm = cbms[c]
            if c == nch - 1 and g > 0:
                out_project(g - 1)
            yoff = _dot(c_g, s_g.astype(bf16))
            xd_parts = []
            for q in range(2):
                h0 = 4 * g + 2 * q
                cols = slice(GW * g + LANES * q, GW * g + LANES * (q + 1))
                xp = xs_scr[rows, cols]
                xpb = xp.astype(bf16)
                ms = []
                for hh in (h0, h0 + 1):
                    seg2 = acs2_l[c][:, hh:hh + 1] - rowt_l[c][hh:hh + 1, :]
                    ms.append((cbm * jnp.exp2(jnp.where(causal, seg2, -jnp.inf))).astype(bf16))
                zb = jnp.zeros_like(xpb)
                bd = jnp.concatenate([jnp.where(lo, xpb, zb), jnp.where(lo, zb, xpb)], axis=0)
                ydiag = _dot(jnp.concatenate(ms, axis=1), bd)
                e_pair = jnp.where(lo, eacs_l[c][:, h0:h0 + 1], eacs_l[c][:, h0 + 1:h0 + 2])
                w_pair = jnp.where(lo, w_l[c][:, h0:h0 + 1], w_l[c][:, h0 + 1:h0 + 2])
                y = ydiag + yoff[:, LANES * q:LANES * (q + 1)] * e_pair + dsk_ref[:, cols] * xp
                yg = y * _silu(z_scr[g, rows, LANES * q:LANES * (q + 1)])
                y_scr[rows, cols] = yg
                ssq[c] = ssq[c] + yg * yg
                xd_parts.append((xp * w_pair).astype(bf16))
            xd = jnp.concatenate(xd_parts, axis=1)
            cd_row = jnp.zeros((1, GW), f32)
            for r in range(4):
                cd_row = jnp.where(lane256 == r, cd_l[c][:, 4 * g + r:4 * g + r + 1], cd_row)
            s_g = s_g * cd_row + lax.dot_general(b_g, xd, TN_DIMS, preferred_element_type=f32)
        state_ref[g] = s_g

    out_project(ng - 1)
    ssq_all = jnp.concatenate(ssq, axis=0)
    rms = lax.rsqrt(jnp.sum(ssq_all, axis=-1, keepdims=True) * (1.0 / SSM_D_INNER) + EPS)
    out = out_scr[...] * rms
    o_ref[0] = x + out * _rms_scale(out) * postw_ref[...]


def _ssd_layer(x, layer, j, pre_norm, post_norm, w_in, conv_w, conv_b, dt_bias, a_log, d_rep,
               gate_norm, w_out):
    bsz, seq, d = x.shape
    ts = SSD_TILE
    nt = seq // ts
    ng = SSM_GROUPS
    x_spec = pl.BlockSpec((1, ts, d), lambda b, t: (b, t, 0))
    zcol, xcol = 0, SSM_D_INNER // GW
    bcol = (2 * SSM_D_INNER) // LANES
    ccol = bcol + SSM_BC_DIM // LANES
    w_specs = ([_col_block_spec(j, d, GW, zcol + g) for g in range(ng)]
               + [_col_block_spec(j, d, GW, xcol + g) for g in range(ng)]
               + [_col_block_spec(j, d, LANES, bcol + g) for g in range(ng)]
               + [_col_block_spec(j, d, LANES, ccol + g) for g in range(ng)]
               + [_const_spec((d, SSM_HEADS))])
    wdt = w_in[j, :, SSM_D_INNER + SSM_CONV_DIM:]
    small = (conv_w, conv_b, dt_bias, a_log, d_rep, gate_norm, w_out)
    in_specs = ([x_spec, _layer_spec(pre_norm, layer), _layer_spec(post_norm, layer)] + w_specs
                + [_layer_spec(a, j) for a in small])
    return pl.pallas_call(
        _ssd_kernel,
        grid=(bsz, nt),
        in_specs=in_specs,
        out_specs=x_spec,
        out_shape=jax.ShapeDtypeStruct(x.shape, f32),
        scratch_shapes=[
            pltpu.VMEM((SSM_CONV_DIM // LANES, ts + SUBLANES, LANES), f32),
            pltpu.VMEM((ts, SSM_D_INNER), f32),
            pltpu.VMEM((ts, 2 * SSM_BC_DIM), bf16),
            pltpu.VMEM((ng, ts, GW), f32),
            pltpu.VMEM((ts, SSM_D_INNER), f32),
            pltpu.VMEM((ts, d), f32),
            pltpu.VMEM((ts, d), bf16),
            pltpu.VMEM((ng, SSM_STATE, 4 * SSM_HEAD_DIM), f32),
        ],
        compiler_params=pltpu.CompilerParams(
            dimension_semantics=("arbitrary", "arbitrary"),
            vmem_limit_bytes=VMEM_LIMIT_BYTES),
        name="ssd_layer",
    )(x, pre_norm, post_norm, *([w_in] * (4 * ng)), wdt, *small)


def _swa_kernel(*refs):
    nqw = ATT_WIDTH // GW
    sink_ref, x_ref, pos_ref, inv_ref, sel_ref, prew_ref, postw_ref = refs[:7]
    wq = refs[7:7 + nqw]
    wk_ref, wv_ref = refs[7 + nqw:9 + nqw]
    wg = refs[9 + nqw:9 + 2 * nqw]
    (wout_ref, o_ref, kpad, vtp, q_scr, o_scr, h_scr, s_scr, g_scr) = refs[9 + 2 * nqw:]
    ts = SWA_TILE
    nqb = ts // ATT_BLOCK
    t = pl.program_id(1)

    @pl.when(t == 0)
    def _():
        kpad[:, 0:ATT_BLOCK, :] = jnp.zeros((kpad.shape[0], ATT_BLOCK, LANES), bf16)
        vtp[...] = jnp.zeros(vtp.shape, bf16)

    x = x_ref[0]
    h_scr[...] = (x * _rms_scale(x) * prew_ref[...]).astype(bf16)
    h = h_scr[...]

    ang = inv_ref[...] * pos_ref[0].astype(f32)
    pieces = []
    for val in (jnp.cos(ang), jnp.sin(ang)):
        hi = val.astype(bf16).astype(f32)
        mid = (val - hi).astype(bf16).astype(f32)
        pieces += [hi, mid, (val - hi - mid).astype(bf16).astype(f32)]
    pieces.append(jnp.zeros((LANES - 8 * len(pieces), ts), f32))
    tabs = _dot(jnp.concatenate(pieces, axis=0).T.astype(bf16), sel_ref[...])
    l64 = lax.broadcasted_iota(jnp.int32, (ts, LANES), 1) % ATT_HEAD_DIM
    hi_half = lax.broadcasted_iota(jnp.int32, (ts, LANES), 1) >= ATT_HEAD_DIM
    cos_t = jnp.where(l64 < ROPE_DIM, tabs[:, :LANES], 1.0)
    sin_t = tabs[:, LANES:]

    def rope(tb, c, s):
        sw = jnp.where(l64 < ROPE_DIM // 2, pltpu.roll(tb, LANES - ROPE_DIM // 2, axis=1),
                       jnp.where(l64 < ROPE_DIM, pltpu.roll(tb, ROPE_DIM // 2, axis=1), 0.0))
        return tb * c + sw * s

    scale = ATT_HEAD_DIM ** -0.5 * LOG2E
    cos_q = cos_t * scale
    sin_q = sin_t * scale
    for jj in range(nqw):
        qb = _dot(h, wq[jj][...])
        for half in range(2):
            cols = slice(GW * jj + LANES * half, GW * jj + LANES * (half + 1))
            q_scr[:, cols] = rope(qb[:, LANES * half:LANES * (half + 1)], cos_q, sin_q).astype(bf16)

    k = _dot(h, wk_ref[...])
    v = _dot(h, wv_ref[...])
    new = slice(ATT_BLOCK, ATT_BLOCK + ts)
    for kb in range(ATT_KV_WIDTH // LANES):
        kblk = rope(k[:, LANES * kb:LANES * (kb + 1)], cos_t, sin_t)
        ksw = pltpu.roll(kblk, ATT_HEAD_DIM, axis=1)
        vt = v[:, LANES * kb:LANES * (kb + 1)].T.astype(bf16)
        for hk in range(2):
            kh = 2 * kb + hk
            for hq in range(2):
                ksrc = kblk if hq == hk else ksw
                keep = hi_half if hq == 1 else jnp.logical_not(hi_half)
                kpad[2 * kh + hq, new, :] = jnp.where(keep, ksrc, 0.0).astype(bf16)
                vtp[2 * kh + hq, ATT_HEAD_DIM * hq:ATT_HEAD_DIM * (hq + 1), new] = (
                    vt[ATT_HEAD_DIM * hk:ATT_HEAD_DIM * (hk + 1), :])

    for i in range(nqb):
        rows = slice(ATT_BLOCK * i, ATT_BLOCK * (i + 1))
        keys = slice(ATT_BLOCK * i, ATT_BLOCK * (i + 2))
        for j in range(ATT_WIDTH // LANES):
            kh = j // 2
            klhs = jnp.concatenate([kpad[2 * kh, keys, :], kpad[2 * kh + 1, keys, :]], axis=0)
            s_scr[(ATT_WIDTH // LANES) * i + j] = lax.dot_general(
                klhs, q_scr[rows, LANES * j:LANES * (j + 1)], NT_DIMS, preferred_element_type=f32)
    for jj in range(nqw):
        g_scr[:, GW * jj:GW * (jj + 1)] = _silu(_dot(h, wg[jj][...]))

    kj = lax.broadcasted_iota(jnp.int32, (2 * ATT_BLOCK, ATT_BLOCK), 0)
    qi = lax.broadcasted_iota(jnp.int32, (2 * ATT_BLOCK, ATT_BLOCK), 1)
    dist = qi + ATT_BLOCK - kj
    in_window = jnp.logical_and(dist >= 0, dist < ATT_BLOCK)
    first_ok = jnp.logical_and(in_window, jnp.logical_or(kj >= ATT_BLOCK, t > 0))
    low_rows = lax.broadcasted_iota(jnp.int32, (LANES, ATT_BLOCK), 0) < ATT_HEAD_DIM
    for i in range(nqb):
        rows = slice(ATT_BLOCK * i, ATT_BLOCK * (i + 1))
        keys = slice(ATT_BLOCK * i, ATT_BLOCK * (i + 2))
        valid = first_ok if i == 0 else in_window
        for j in range(ATT_WIDTH // LANES):
            kh = j // 2
            ps, rdens = [], []
            for hq in range(2):
                sink2 = sink_ref[2 * j + hq] * LOG2E
                s = s_scr[(ATT_WIDTH // LANES) * i + j, 2 * ATT_BLOCK * hq:2 * ATT_BLOCK * (hq + 1), :]
                s = jnp.where(valid, s, -jnp.inf)
                m = jnp.maximum(jnp.max(s, axis=0, keepdims=True), sink2)
                p = jnp.exp2(s - m)
                rdens.append(1.0 / (jnp.sum(p, axis=0, keepdims=True) + jnp.exp2(sink2 - m)))
                ps.append(p.astype(bf16))
            vcat = jnp.concatenate([vtp[2 * kh, :, keys], vtp[2 * kh + 1, :, keys]], axis=1)
            ot = _dot(vcat, jnp.concatenate(ps, axis=0))
            ot = ot * jnp.where(low_rows, rdens[0], rdens[1])
            o_scr[rows, LANES * j:LANES * (j + 1)] = ot.T

    kpad[:, 0:ATT_BLOCK, :] = kpad[:, ts:ts + ATT_BLOCK, :]
    vtp[:, :, 0:ATT_BLOCK] = vtp[:, :, ts:ts + ATT_BLOCK]

    og = (o_scr[...] * g_scr[...]).astype(bf16)
    out = _dot(og, wout_ref[...])
    o_ref[0] = x + out * _rms_scale(out) * postw_ref[...]


def _rope_selector():
    sel = np.zeros((LANES, 2 * LANES), np.float32)
    half = ROPE_DIM // 2
    for r in range(6 * half):
        kind, f = r // (3 * half), r % half
        for lane in range(LANES):
            d = lane % ATT_HEAD_DIM
            if d < ROPE_DIM and d % half == f:
                sel[r, LANES * kind + lane] = -1.0 if (kind == 1 and d < half) else 1.0
    return jnp.asarray(sel, bf16)


def _swa_layer(x, positions, layer, j, pre_norm, post_norm, w_in, sinks, w_out):
    bsz, seq, d = x.shape
    ts = SWA_TILE
    nt = seq // ts
    nqw = ATT_WIDTH // GW
    x_spec = pl.BlockSpec((1, ts, d), lambda b, t: (b, t, 0))
    pos_spec = pl.BlockSpec((1, 1, ts), lambda b, t: (b * nt + t, 0, 0))
    kcol = ATT_WIDTH // GW
    gcol = (ATT_WIDTH + 2 * ATT_KV_WIDTH) // GW
    w_specs = ([_col_block_spec(j, d, GW, jj) for jj in range(nqw)]
               + [_col_block_spec(j, d, GW, kcol), _col_block_spec(j, d, GW, kcol + 1)]
               + [_col_block_spec(j, d, GW, gcol + jj) for jj in range(nqw)])
    inv = ROPE_THETA ** (-jnp.arange(0, ROPE_DIM, 2, dtype=f32) / ROPE_DIM)
    consts = (jnp.broadcast_to(inv[:, None], (ROPE_DIM // 2, ts)), _rope_selector())
    in_specs = ([pl.BlockSpec(memory_space=pltpu.SMEM), x_spec, pos_spec]
                + [_const_spec(a.shape) for a in consts]
                + [_layer_spec(pre_norm, layer), _layer_spec(post_norm, layer)] + w_specs
                + [_layer_spec(w_out, j)])
    return pl.pallas_call(
        _swa_kernel,
        grid=(bsz, nt),
        in_specs=in_specs,
        out_specs=x_spec,
        out_shape=jax.ShapeDtypeStruct(x.shape, f32),
        scratch_shapes=[
            pltpu.VMEM((2 * ATT_KV_HEADS, ts + ATT_BLOCK, LANES), bf16),
            pltpu.VMEM((2 * ATT_KV_HEADS, LANES, ts + ATT_BLOCK), bf16),
            pltpu.VMEM((ts, ATT_WIDTH), bf16),
            pltpu.VMEM((ts, ATT_WIDTH), f32),
            pltpu.VMEM((ts, d), bf16),
            pltpu.VMEM((ATT_WIDTH // LANES * ts // ATT_BLOCK, 4 * ATT_BLOCK, ATT_BLOCK), f32),
            pltpu.VMEM((ts, ATT_WIDTH), f32),
        ],
        compiler_params=pltpu.CompilerParams(
            dimension_semantics=("arbitrary", "arbitrary"),
            vmem_limit_bytes=VMEM_LIMIT_BYTES),
        name="swa_layer",
    )(sinks[j], x, positions.reshape(bsz * nt, 1, ts), *consts, pre_norm, post_norm,
      *([w_in] * (2 * nqw + 2)), w_out)


def kernel(x, positions, pre_norm, post_norm, ssm_w_in, ssm_conv_w, ssm_conv_b, ssm_dt_bias, ssm_a_log, ssm_d, ssm_gate_norm, ssm_w_out, att_w_in, att_sinks, att_w_out):
    depth = pre_norm.shape[0]

    def rows(a):
        return a.reshape(a.shape[0], 1, a.shape[1])

    pre_norm, post_norm = rows(pre_norm), rows(post_norm)
    ssm = (ssm_w_in.astype(bf16), ssm_conv_w, rows(ssm_conv_b), rows(ssm_dt_bias), rows(ssm_a_log),
           rows(jnp.repeat(ssm_d, SSM_HEAD_DIM, axis=1)), rows(ssm_gate_norm), ssm_w_out.astype(bf16))
    att_w_in, att_w_out = att_w_in.astype(bf16), att_w_out.astype(bf16)
    for i in range(depth):
        j = i // 2
        if i % 2 == 0:
            x = _ssd_layer(x, i, j, pre_norm, post_norm, *ssm)
        else:
            x = _swa_layer(x, positions, i, j, pre_norm, post_norm, att_w_in, att_sinks, att_w_out)
    return x
```

```python
import jax
import jax.numpy as jnp
import numpy as np
from jax import lax
from jax.experimental import pallas as pl
from jax.experimental.pallas import tpu as pltpu

f32 = jnp.float32
bf16 = jnp.bfloat16

EPS = 1e-6
LOG2E = 1.4426950408889634
LANES = 128
SUBLANES = 8
VMEM_LIMIT_BYTES = 56 * 1024 * 1024

D_MODEL = 1024
SSM_D_INNER = 2048
SSM_HEAD_DIM = 64
SSM_HEADS = 32
SSM_GROUPS = 8
SSM_STATE = 128
SSM_CONV = 4
SSM_CHUNK = 128
SSM_BC_DIM = SSM_GROUPS * SSM_STATE
SSM_CONV_DIM = SSM_D_INNER + 2 * SSM_BC_DIM
ATT_HEAD_DIM = 64
ATT_Q_HEADS = 16
ATT_KV_HEADS = 4
ATT_WIDTH = 1024
ATT_KV_WIDTH = 256
ATT_BLOCK = 128
ROPE_THETA = 500000.0
ROPE_DIM = 16

SSD_TILE = 256
SWA_TILE = 512
GW = 2 * LANES

NT_DIMS = (((1,), (1,)), ((), ()))
TN_DIMS = (((0,), (0,)), ((), ()))


def _dot(a, b):
    return jnp.dot(a, b, preferred_element_type=f32)


def _rms_scale(v):
    return lax.rsqrt(jnp.sum(v * v, axis=-1, keepdims=True) * (1.0 / v.shape[-1]) + EPS)


def _silu(v):
    return v * jax.nn.sigmoid(v)


def _const_spec(shape):
    nd = len(shape)
    return pl.BlockSpec(shape, lambda b, t: (0,) * nd, pipeline_mode=pl.Buffered(1))


def _layer_spec(arr, layer):
    rest = arr.shape[1:]
    return pl.BlockSpec((None,) + rest, lambda b, t: (layer,) + (0,) * len(rest),
                        pipeline_mode=pl.Buffered(1))


def _col_block_spec(layer, rows, width, index):
    return pl.BlockSpec((None, rows, width), lambda b, t: (layer, 0, index),
                        pipeline_mode=pl.Buffered(1))


def _split3(v):
    hi = v.astype(bf16)
    r1 = v - hi.astype(f32)
    mid = r1.astype(bf16)
    return hi, mid, (r1 - mid.astype(f32)).astype(bf16)


def _ssd_kernel(*refs):
    ng = SSM_GROUPS
    x_ref, prew_ref, postw_ref = refs[:3]
    wz, wxs, wb, wc = (refs[3 + ng * i:3 + ng * (i + 1)] for i in range(4))
    (wdt_ref, cw_ref, cb_ref, dtb_ref, alog_ref, dsk_ref, gn_ref, wout_ref, o_ref,
     cbuf, xs_scr, bc_scr, z_scr, y_scr, out_scr, h_scr, state_ref) = refs[3 + 4 * ng:]
    ts = SSD_TILE
    nch = ts // SSM_CHUNK
    t = pl.program_id(1)

    @pl.when(t == 0)
    def _():
        cbuf[:, 0:SUBLANES, :] = jnp.zeros((cbuf.shape[0], SUBLANES, LANES), f32)
        state_ref[...] = jnp.zeros(state_ref.shape, f32)

    x = x_ref[0]
    h_scr[...] = (x * _rms_scale(x) * prew_ref[...]).astype(bf16)

    def conv_cols(g, part, half):
        if part == 0:
            start = GW * g + LANES * half
        else:
            start = SSM_D_INNER + SSM_BC_DIM * half + LANES * g
        return slice(start, start + LANES)

    def project(g):
        h = h_scr[...]
        raws = (_dot(h, wxs[g][...]),
                _dot(h, jnp.concatenate([wb[g][...], wc[g][...]], axis=1)))
        for part in range(2):
            for half in range(2):
                cbuf[4 * g + 2 * part + half, SUBLANES:SUBLANES + ts, :] = (
                    raws[part][:, LANES * half:LANES * (half + 1)])
        z_scr[g] = _dot(h, wz[g][...])

    def conv(g):
        for part in range(2):
            for half in range(2):
                i = 4 * g + 2 * part + half
                cols = conv_cols(g, part, half)
                acc = cb_ref[:, cols] + cbuf[i, pl.ds(SUBLANES - 3, ts), :] * cw_ref[0:1, cols]
                for k in range(1, SSM_CONV):
                    acc = acc + cbuf[i, pl.ds(SUBLANES - 3 + k, ts), :] * cw_ref[k:k + 1, cols]
                act = _silu(acc)
                cbuf[i, 0:SUBLANES, :] = cbuf[i, ts:ts + SUBLANES, :]
                dst = slice(GW * g + LANES * half, GW * g + LANES * (half + 1))
                if part == 0:
                    xs_scr[:, dst] = act
                else:
                    bc_scr[:, dst] = act.astype(bf16)

    def out_project(g):
        cols = slice(GW * g, GW * (g + 1))
        part = _dot((y_scr[:, cols] * gn_ref[:, cols]).astype(bf16), wout_ref[cols, :])
        if g == 0:
            out_scr[...] = part
        else:
            out_scr[...] += part

    project(0)
    conv(0)
    project(1)

    dt = jax.nn.softplus(_dot(h_scr[...], wdt_ref[...]) + dtb_ref[...])
    a = dt * (-jnp.exp(alog_ref[...]))
    r_i = lax.broadcasted_iota(jnp.int32, (SSM_CHUNK, SSM_CHUNK), 0)
    c_i = lax.broadcasted_iota(jnp.int32, (SSM_CHUNK, SSM_CHUNK), 1)
    causal = r_i >= c_i
    ltri = causal.astype(bf16)
    ltri3 = jnp.concatenate([ltri, ltri, ltri], axis=1)
    lo = c_i < SSM_HEAD_DIM
    acs2_l, rowt_l, eacs_l, w_l, cd_l = [], [], [], [], []
    for c in range(nch):
        rows = slice(SSM_CHUNK * c, SSM_CHUNK * (c + 1))
        acs = _dot(ltri3, jnp.concatenate(_split3(a[rows]), axis=0))
        last = acs[SSM_CHUNK - 1:SSM_CHUNK, :]
        acs2 = acs * LOG2E
        acs2_l.append(acs2)
        rowt_l.append((acs2 - jnp.log(dt[rows]) * LOG2E).T)
        eacs_l.append(jnp.exp(acs))
        w_l.append(dt[rows] * jnp.exp(last - acs))
        cd_l.append(jnp.exp(last))

    lane256 = lax.broadcasted_iota(jnp.int32, (1, GW), 1) // SSM_HEAD_DIM
    ssq = [jnp.zeros((SSM_CHUNK, LANES), f32) for _ in range(nch)]
    for g in range(ng):
        s_g = state_ref[g]
        cbms = []
        for c in range(nch):
            rows = slice(SSM_CHUNK * c, SSM_CHUNK * (c + 1))
            cbms.append(lax.dot_general(bc_scr[rows, GW * g + LANES:GW * (g + 1)],
                                        bc_scr[rows, GW * g:GW * g + LANES],
                                        NT_DIMS, preferred_element_type=f32))
        if g + 1 < ng:
            conv(g + 1)
        if g + 2 < ng:
            project(g + 2)
        for c in range(nch):
            rows = slice(SSM_CHUNK * c, SSM_CHUNK * (c + 1))
            b_g = bc_scr[rows, GW * g:GW * g + LANES]
            c_g = bc_scr[rows, GW * g + LANES:GW * (g + 1)]
            cbm = cbms[c]
            if c == nch - 1 and g > 0:
                out_project(g - 1)
            yoff = _dot(c_g, s_g.astype(bf16))
            xd_parts = []
            for q in range(2):
                h0 = 4 * g + 2 * q
                cols = slice(GW * g + LANES * q, GW * g + LANES * (q + 1))
                xp = xs_scr[rows, cols]
                xpb = xp.astype(bf16)
                ms = []
                for hh in (h0, h0 + 1):
                    seg2 = acs2_l[c][:, hh:hh + 1] - rowt_l[c][hh:hh + 1, :]
                    ms.append((cbm * jnp.exp2(jnp.where(causal, seg2, -jnp.inf))).astype(bf16))
                zb = jnp.zeros_like(xpb)
                bd = jnp.concatenate([jnp.where(lo, xpb, zb), jnp.where(lo, zb, xpb)], axis=0)
                ydiag = _dot(jnp.concatenate(ms, axis=1), bd)
                e_pair = jnp.where(lo, eacs_l[c][:, h0:h0 + 1], eacs_l[c][:, h0 + 1:h0 + 2])
                w_pair = jnp.where(lo, w_l[c][:, h0:h0 + 1], w_l[c][:, h0 + 1:h0 + 2])
                y = ydiag + yoff[:, LANES * q:LANES * (q + 1)] * e_pair + dsk_ref[:, cols] * xp
                yg = y * _silu(z_scr[g, rows, LANES * q:LANES * (q + 1)])
                y_scr[rows, cols] = yg
                ssq[c] = ssq[c] + yg * yg
                xd_parts.append((xp * w_pair).astype(bf16))
            xd = jnp.concatenate(xd_parts, axis=1)
            cd_row = jnp.zeros((1, GW), f32)
            for r in range(4):
                cd_row = jnp.where(lane256 == r, cd_l[c][:, 4 * g + r:4 * g + r + 1], cd_row)
            s_g = s_g * cd_row + lax.dot_general(b_g, xd, TN_DIMS, preferred_element_type=f32)
        state_ref[g] = s_g

    out_project(ng - 1)
    ssq_all = jnp.concatenate(ssq, axis=0)
    rms = lax.rsqrt(jnp.sum(ssq_all, axis=-1, keepdims=True) * (1.0 / SSM_D_INNER) + EPS)
    out = out_scr[...] * rms
    o_ref[0] = x + out * _rms_scale(out) * postw_ref[...]


def _ssd_layer(x, layer, j, pre_norm, post_norm, w_in, conv_w, conv_b, dt_bias, a_log, d_rep,
               gate_norm, w_out):
    bsz, seq, d = x.shape
    ts = SSD_TILE
    nt = seq // ts
    ng = SSM_GROUPS
    x_spec = pl.BlockSpec((1, ts, d), lambda b, t: (b, t, 0))
    zcol, xcol = 0, SSM_D_INNER // GW
    bcol = (2 * SSM_D_INNER) // LANES
    ccol = bcol + SSM_BC_DIM // LANES
    w_specs = ([_col_block_spec(j, d, GW, zcol + g) for g in range(ng)]
               + [_col_block_spec(j, d, GW, xcol + g) for g in range(ng)]
               + [_col_block_spec(j, d, LANES, bcol + g) for g in range(ng)]
               + [_col_block_spec(j, d, LANES, ccol + g) for g in range(ng)]
               + [_const_spec((d, SSM_HEADS))])
    wdt = w_in[j, :, SSM_D_INNER + SSM_CONV_DIM:]
    small = (conv_w, conv_b, dt_bias, a_log, d_rep, gate_norm, w_out)
    in_specs = ([x_spec, _layer_spec(pre_norm, layer), _layer_spec(post_norm, layer)] + w_specs
                + [_layer_spec(a, j) for a in small])
    return pl.pallas_call(
        _ssd_kernel,
        grid=(bsz, nt),
        in_specs=in_specs,
        out_specs=x_spec,
        out_shape=jax.ShapeDtypeStruct(x.shape, f32),
        scratch_shapes=[
            pltpu.VMEM((SSM_CONV_DIM // LANES, ts + SUBLANES, LANES), f32),
            pltpu.VMEM((ts, SSM_D_INNER), f32),
            pltpu.VMEM((ts, 2 * SSM_BC_DIM), bf16),
            pltpu.VMEM((ng, ts, GW), f32),
            pltpu.VMEM((ts, SSM_D_INNER), f32),
            pltpu.VMEM((ts, d), f32),
            pltpu.VMEM((ts, d), bf16),
            pltpu.VMEM((ng, SSM_STATE, 4 * SSM_HEAD_DIM), f32),
        ],
        compiler_params=pltpu.CompilerParams(
            dimension_semantics=("arbitrary", "arbitrary"),
            vmem_limit_bytes=VMEM_LIMIT_BYTES),
        name="ssd_layer",
    )(x, pre_norm, post_norm, *([w_in] * (4 * ng)), wdt, *small)


def _swa_kernel(*refs):
    nqw = ATT_WIDTH // GW
    sink_ref, x_ref, pos_ref, inv_ref, sel_ref, prew_ref, postw_ref = refs[:7]
    wq = refs[7:7 + nqw]
    wk_ref, wv_ref = refs[7 + nqw:9 + nqw]
    wg = refs[9 + nqw:9 + 2 * nqw]
    (wout_ref, o_ref, kpad, vtp, q_scr, o_scr, h_scr, s_scr, g_scr) = refs[9 + 2 * nqw:]
    ts = SWA_TILE
    nqb = ts // ATT_BLOCK
    t = pl.program_id(1)

    @pl.when(t == 0)
    def _():
        kpad[:, 0:ATT_BLOCK, :] = jnp.zeros((kpad.shape[0], ATT_BLOCK, LANES), bf16)
        vtp[...] = jnp.zeros(vtp.shape, bf16)

    x = x_ref[0]
    h_scr[...] = (x * _rms_scale(x) * prew_ref[...]).astype(bf16)
    h = h_scr[...]

    ang = inv_ref[...] * pos_ref[0].astype(f32)
    pieces = []
    for val in (jnp.cos(ang), jnp.sin(ang)):
        hi = val.astype(bf16).astype(f32)
        mid = (val - hi).astype(bf16).astype(f32)
        pieces += [hi, mid, (val - hi - mid).astype(bf16).astype(f32)]
    pieces.append(jnp.zeros((LANES - 8 * len(pieces), ts), f32))
    tabs = _dot(jnp.concatenate(pieces, axis=0).T.astype(bf16), sel_ref[...])
    l64 = lax.broadcasted_iota(jnp.int32, (ts, LANES), 1) % ATT_HEAD_DIM
    hi_half = lax.broadcasted_iota(jnp.int32, (ts, LANES), 1) >= ATT_HEAD_DIM
    cos_t = jnp.where(l64 < ROPE_DIM, tabs[:, :LANES], 1.0)
    sin_t = tabs[:, LANES:]

    def rope(tb, c, s):
        sw = jnp.where(l64 < ROPE_DIM // 2, pltpu.roll(tb, LANES - ROPE_DIM // 2, axis=1),
                       jnp.where(l64 < ROPE_DIM, pltpu.roll(tb, ROPE_DIM // 2, axis=1), 0.0))
        return tb * c + sw * s

    scale = ATT_HEAD_DIM ** -0.5 * LOG2E
    cos_q = cos_t * scale
    sin_q = sin_t * scale
    for jj in range(nqw):
        qb = _dot(h, wq[jj][...])
        for half in range(2):
            cols = slice(GW * jj + LANES * half, GW * jj + LANES * (half + 1))
            q_scr[:, cols] = rope(qb[:, LANES * half:LANES * (half + 1)], cos_q, sin_q).astype(bf16)

    k = _dot(h, wk_ref[...])
    v = _dot(h, wv_ref[...])
    new = slice(ATT_BLOCK, ATT_BLOCK + ts)
    for kb in range(ATT_KV_WIDTH // LANES):
        kblk = rope(k[:, LANES * kb:LANES * (kb + 1)], cos_t, sin_t)
        ksw = pltpu.roll(kblk, ATT_HEAD_DIM, axis=1)
        vt = v[:, LANES * kb:LANES * (kb + 1)].T.astype(bf16)
        for hk in range(2):
            kh = 2 * kb + hk
            for hq in range(2):
                ksrc = kblk if hq == hk else ksw
                keep = hi_half if hq == 1 else jnp.logical_not(hi_half)
                kpad[2 * kh + hq, new, :] = jnp.where(keep, ksrc, 0.0).astype(bf16)
                vtp[2 * kh + hq, ATT_HEAD_DIM * hq:ATT_HEAD_DIM * (hq + 1), new] = (
                    vt[ATT_HEAD_DIM * hk:ATT_HEAD_DIM * (hk + 1), :])

    for i in range(nqb):
        rows = slice(ATT_BLOCK * i, ATT_BLOCK * (i + 1))
        keys = slice(ATT_BLOCK * i, ATT_BLOCK * (i + 2))
        for j in range(ATT_WIDTH // LANES):
            kh = j // 2
            klhs = jnp.concatenate([kpad[2 * kh, keys, :], kpad[2 * kh + 1, keys, :]], axis=0)
            s_scr[(ATT_WIDTH // LANES) * i + j] = lax.dot_general(
                klhs, q_scr[rows, LANES * j:LANES * (j + 1)], NT_DIMS, preferred_element_type=f32)
    for jj in range(nqw):
        g_scr[:, GW * jj:GW * (jj + 1)] = _silu(_dot(h, wg[jj][...]))

    kj = lax.broadcasted_iota(jnp.int32, (2 * ATT_BLOCK, ATT_BLOCK), 0)
    qi = lax.broadcasted_iota(jnp.int32, (2 * ATT_BLOCK, ATT_BLOCK), 1)
    dist = qi + ATT_BLOCK - kj
    in_window = jnp.logical_and(dist >= 0, dist < ATT_BLOCK)
    first_ok = jnp.logical_and(in_window, jnp.logical_or(kj >= ATT_BLOCK, t > 0))
    low_rows = lax.broadcasted_iota(jnp.int32, (LANES, ATT_BLOCK), 0) < ATT_HEAD_DIM
    for i in range(nqb):
        rows = slice(ATT_BLOCK * i, ATT_BLOCK * (i + 1))
        keys = slice(ATT_BLOCK * i, ATT_BLOCK * (i + 2))
        valid = first_ok if i == 0 else in_window
        for j in range(ATT_WIDTH // LANES):
            kh = j // 2
            ps, rdens = [], []
            for hq in range(2):
                sink2 = sink_ref[2 * j + hq] * LOG2E
                s = s_scr[(ATT_WIDTH // LANES) * i + j, 2 * ATT_BLOCK * hq:2 * ATT_BLOCK * (hq + 1), :]
                s = jnp.where(valid, s, -jnp.inf)
                m = jnp.maximum(jnp.max(s, axis=0, keepdims=True), sink2)
                p = jnp.exp2(s - m)
                rdens.append(1.0 / (jnp.sum(p, axis=0, keepdims=True) + jnp.exp2(sink2 - m)))
                ps.append(p.astype(bf16))
            vcat = jnp.concatenate([vtp[2 * kh, :, keys], vtp[2 * kh + 1, :, keys]], axis=1)
            ot = _dot(vcat, jnp.concatenate(ps, axis=0))
            ot = ot * jnp.where(low_rows, rdens[0], rdens[1])
            o_scr[rows, LANES * j:LANES * (j + 1)] = ot.T

    kpad[:, 0:ATT_BLOCK, :] = kpad[:, ts:ts + ATT_BLOCK, :]
    vtp[:, :, 0:ATT_BLOCK] = vtp[:, :, ts:ts + ATT_BLOCK]

    og = (o_scr[...] * g_scr[...]).astype(bf16)
    out = _dot(og, wout_ref[...])
    o_ref[0] = x + out * _rms_scale(out) * postw_ref[...]


def _rope_selector():
    sel = np.zeros((LANES, 2 * LANES), np.float32)
    half = ROPE_DIM // 2
    for r in range(6 * half):
        kind, f = r // (3 * half), r % half
        for lane in range(LANES):
            d = lane % ATT_HEAD_DIM
            if d < ROPE_DIM and d % half == f:
                sel[r, LANES * kind + lane] = -1.0 if (kind == 1 and d < half) else 1.0
    return jnp.asarray(sel, bf16)


def _swa_layer(x, positions, layer, j, pre_norm, post_norm, w_in, sinks, w_out):
    bsz, seq, d = x.shape
    ts = SWA_TILE
    nt = seq // ts
    nqw = ATT_WIDTH // GW
    x_spec = pl.BlockSpec((1, ts, d), lambda b, t: (b, t, 0))
    pos_spec = pl.BlockSpec((1, 1, ts), lambda b, t: (b * nt + t, 0, 0))
    kcol = ATT_WIDTH // GW
    gcol = (ATT_WIDTH + 2 * ATT_KV_WIDTH) // GW
    w_specs = ([_col_block_spec(j, d, GW, jj) for jj in range(nqw)]
               + [_col_block_spec(j, d, GW, kcol), _col_block_spec(j, d, GW, kcol + 1)]
               + [_col_block_spec(j, d, GW, gcol + jj) for jj in range(nqw)])
    inv = ROPE_THETA ** (-jnp.arange(0, ROPE_DIM, 2, dtype=f32) / ROPE_DIM)
    consts = (jnp.broadcast_to(inv[:, None], (ROPE_DIM // 2, ts)), _rope_selector())
    in_specs = ([pl.BlockSpec(memory_space=pltpu.SMEM), x_spec, pos_spec]
                + [_const_spec(a.shape) for a in consts]
                + [_layer_spec(pre_norm, layer), _layer_spec(post_norm, layer)] + w_specs
                + [_layer_spec(w_out, j)])
    return pl.pallas_call(
        _swa_kernel,
        grid=(bsz, nt),
        in_specs=in_specs,
        out_specs=x_spec,
        out_shape=jax.ShapeDtypeStruct(x.shape, f32),
        scratch_shapes=[
            pltpu.VMEM((2 * ATT_KV_HEADS, ts + ATT_BLOCK, LANES), bf16),
            pltpu.VMEM((2 * ATT_KV_HEADS, LANES, ts + ATT_BLOCK), bf16),
            pltpu.VMEM((ts, ATT_WIDTH), bf16),
            pltpu.VMEM((ts, ATT_WIDTH), f32),
            pltpu.VMEM((ts, d), bf16),
            pltpu.VMEM((ATT_WIDTH // LANES * ts // ATT_BLOCK, 4 * ATT_BLOCK, ATT_BLOCK), f32),
            pltpu.VMEM((ts, ATT_WIDTH), f32),
        ],
        compiler_params=pltpu.CompilerParams(
            dimension_semantics=("arbitrary", "arbitrary"),
            vmem_limit_bytes=VMEM_LIMIT_BYTES),
        name="swa_layer",
    )(sinks[j], x, positions.reshape(bsz * nt, 1, ts), *consts, pre_norm, post_norm,
      *([w_in] * (2 * nqw + 2)), w_out)


def kernel(x, positions, pre_norm, post_norm, ssm_w_in, ssm_conv_w, ssm_conv_b, ssm_dt_bias, ssm_a_log, ssm_d, ssm_gate_norm, ssm_w_out, att_w_in, att_sinks, att_w_out):
    depth = pre_norm.shape[0]

    def rows(a):
        return a.reshape(a.shape[0], 1, a.shape[1])

    pre_norm, post_norm = rows(pre_norm), rows(post_norm)
    ssm = (ssm_w_in.astype(bf16), ssm_conv_w, rows(ssm_conv_b), rows(ssm_dt_bias), rows(ssm_a_log),
           rows(jnp.repeat(ssm_d, SSM_HEAD_DIM, axis=1)), rows(ssm_gate_norm), ssm_w_out.astype(bf16))
    att_w_in, att_w_out = att_w_in.astype(bf16), att_w_out.astype(bf16)
    for i in range(depth):
        j = i // 2
        if i % 2 == 0:
            x = _ssd_layer(x, i, j, pre_norm, post_norm, *ssm)
        else:
            x = _swa_layer(x, positions, i, j, pre_norm, post_norm, att_w_in, att_sinks, att_w_out)
    return x
```

```python
import jax
import jax.numpy as jnp
import numpy as np
from jax import lax
from jax.experimental import pallas as pl
from jax.experimental.pallas import tpu as pltpu

f32 = jnp.float32
bf16 = jnp.bfloat16

EPS = 1e-6
LOG2E = 1.4426950408889634
LANES = 128
SUBLANES = 8
VMEM_LIMIT_BYTES = 56 * 1024 * 1024

D_MODEL = 1024
SSM_D_INNER = 2048
SSM_HEAD_DIM = 64
SSM_HEADS = 32
SSM_GROUPS = 8
SSM_STATE = 128
SSM_CONV = 4
SSM_CHUNK = 128
SSM_BC_DIM = SSM_GROUPS * SSM_STATE
SSM_CONV_DIM = SSM_D_INNER + 2 * SSM_BC_DIM
ATT_HEAD_DIM = 64
ATT_Q_HEADS = 16
ATT_KV_HEADS = 4
ATT_WIDTH = 1024
ATT_KV_WIDTH = 256
ATT_BLOCK = 128
ROPE_THETA = 500000.0
ROPE_DIM = 16

SSD_TILE = 256
SWA_TILE = 512
GW = 2 * LANES

NT_DIMS = (((1,), (1,)), ((), ()))
TN_DIMS = (((0,), (0,)), ((), ()))


def _dot(a, b):
    return jnp.dot(a, b, preferred_element_type=f32)


def _rms_scale(v):
    return lax.rsqrt(jnp.sum(v * v, axis=-1, keepdims=True) * (1.0 / v.shape[-1]) + EPS)


def _silu(v):
    return v * jax.nn.sigmoid(v)


def _const_spec(shape):
    nd = len(shape)
    return pl.BlockSpec(shape, lambda b, t: (0,) * nd, pipeline_mode=pl.Buffered(1))


def _layer_spec(arr, layer):
    rest = arr.shape[1:]
    return pl.BlockSpec((None,) + rest, lambda b, t: (layer,) + (0,) * len(rest),
                        pipeline_mode=pl.Buffered(1))


def _col_block_spec(layer, rows, width, index):
    return pl.BlockSpec((None, rows, width), lambda b, t: (layer, 0, index),
                        pipeline_mode=pl.Buffered(1))


def _split3(v):
    hi = v.astype(bf16)
    r1 = v - hi.astype(f32)
    mid = r1.astype(bf16)
    return hi, mid, (r1 - mid.astype(f32)).astype(bf16)


def _ssd_kernel(*refs):
    ng = SSM_GROUPS
    x_ref, prew_ref, postw_ref = refs[:3]
    wz, wxs, wb, wc = (refs[3 + ng * i:3 + ng * (i + 1)] for i in range(4))
    (wdt_ref, cw_ref, cb_ref, dtb_ref, alog_ref, dsk_ref, gn_ref, wout_ref, o_ref,
     cbuf, xs_scr, bc_scr, z_scr, y_scr, out_scr, h_scr, state_ref) = refs[3 + 4 * ng:]
    ts = SSD_TILE
    nch = ts // SSM_CHUNK
    t = pl.program_id(1)

    @pl.when(t == 0)
    def _():
        cbuf[:, 0:SUBLANES, :] = jnp.zeros((cbuf.shape[0], SUBLANES, LANES), f32)
        state_ref[...] = jnp.zeros(state_ref.shape, f32)

    x = x_ref[0]
    h_scr[...] = (x * _rms_scale(x) * prew_ref[...]).astype(bf16)

    def conv_cols(g, part, half):
        if part == 0:
            start = GW * g + LANES * half
        else:
            start = SSM_D_INNER + SSM_BC_DIM * half + LANES * g
        return slice(start, start + LANES)

    def project(g):
        h = h_scr[...]
        raws = (_dot(h, wxs[g][...]),
                _dot(h, jnp.concatenate([wb[g][...], wc[g][...]], axis=1)))
        for part in range(2):
            for half in range(2):
                cbuf[4 * g + 2 * part + half, SUBLANES:SUBLANES + ts, :] = (
                    raws[part][:, LANES * half:LANES * (half + 1)])
        z_scr[g] = _dot(h, wz[g][...])

    def conv(g):
        for part in range(2):
            for half in range(2):
                i = 4 * g + 2 * part + half
                cols = conv_cols(g, part, half)
                acc = cb_ref[:, cols] + cbuf[i, pl.ds(SUBLANES - 3, ts), :] * cw_ref[0:1, cols]
                for k in range(1, SSM_CONV):
                    acc = acc + cbuf[i, pl.ds(SUBLANES - 3 + k, ts), :] * cw_ref[k:k + 1, cols]
                act = _silu(acc)
                cbuf[i, 0:SUBLANES, :] = cbuf[i, ts:ts + SUBLANES, :]
                dst = slice(GW * g + LANES * half, GW * g + LANES * (half + 1))
                if part == 0:
                    xs_scr[:, dst] = act
                else:
                    bc_scr[:, dst] = act.astype(bf16)

    def out_project(g):
        cols = slice(GW * g, GW * (g + 1))
        part = _dot((y_scr[:, cols] * gn_ref[:, cols]).astype(bf16), wout_ref[cols, :])
        if g == 0:
            out_scr[...] = part
        else:
            out_scr[...] += part

    project(0)
    conv(0)
    project(1)

    dt = jax.nn.softplus(_dot(h_scr[...], wdt_ref[...]) + dtb_ref[...])
    a = dt * (-jnp.exp(alog_ref[...]))
    r_i = lax.broadcasted_iota(jnp.int32, (SSM_CHUNK, SSM_CHUNK), 0)
    c_i = lax.broadcasted_iota(jnp.int32, (SSM_CHUNK, SSM_CHUNK), 1)
    causal = r_i >= c_i
    ltri = causal.astype(bf16)
    ltri3 = jnp.concatenate([ltri, ltri, ltri], axis=1)
    lo = c_i < SSM_HEAD_DIM
    acs2_l, rowt_l, eacs_l, w_l, cd_l = [], [], [], [], []
    for c in range(nch):
        rows = slice(SSM_CHUNK * c, SSM_CHUNK * (c + 1))
        acs = _dot(ltri3, jnp.concatenate(_split3(a[rows]), axis=0))
        last = acs[SSM_CHUNK - 1:SSM_CHUNK, :]
        acs2 = acs * LOG2E
        acs2_l.append(acs2)
        rowt_l.append((acs2 - jnp.log(dt[rows]) * LOG2E).T)
        eacs_l.append(jnp.exp(acs))
        w_l.append(dt[rows] * jnp.exp(last - acs))
        cd_l.append(jnp.exp(last))

    lane256 = lax.broadcasted_iota(jnp.int32, (1, GW), 1) // SSM_HEAD_DIM
    ssq = [jnp.zeros((SSM_CHUNK, LANES), f32) for _ in range(nch)]
    for g in range(ng):
        s_g = state_ref[g]
        cbms = []
        for c in range(nch):
            rows = slice(SSM_CHUNK * c, SSM_CHUNK * (c + 1))
            cbms.append(lax.dot_general(bc_scr[rows, GW * g + LANES:GW * (g + 1)],
                                        bc_scr[rows, GW * g:GW * g + LANES],
                                        NT_DIMS, preferred_element_type=f32))
        if g + 1 < ng:
            conv(g + 1)
        if g + 2 < ng:
            project(g + 2)
        for c in range(nch):
            rows = slice(SSM_CHUNK * c, SSM_CHUNK * (c + 1))
            b_g = bc_scr[rows, GW * g:GW * g + LANES]
            c_g = bc_scr[rows, GW * g + LANES:GW * (g + 1)]
            cbm = cbms[c]
            if c == nch - 1 and g > 0:
                out_project(g - 1)
            yoff = _dot(c_g, s_g.astype(bf16))
            xd_parts = []
            for q in range(2):
                h0 = 4 * g + 2 * q
                cols = slice(GW * g + LANES * q, GW * g + LANES * (q + 1))
                xp = xs_scr[rows, cols]
                xpb = xp.astype(bf16)
                ms = []
                for hh in (h0, h0 + 1):
                    seg2 = acs2_l[c][:, hh:hh + 1] - rowt_l[c][hh:hh + 1, :]
                    ms.append((cbm * jnp.exp2(jnp.where(causal, seg2, -jnp.inf))).astype(bf16))
                zb = jnp.zeros_like(xpb)
                bd = jnp.concatenate([jnp.where(lo, xpb, zb), jnp.where(lo, zb, xpb)], axis=0)
                ydiag = _dot(jnp.concatenate(ms, axis=1), bd)
                e_pair = jnp.where(lo, eacs_l[c][:, h0:h0 + 1], eacs_l[c][:, h0 + 1:h0 + 2])
                w_pair = jnp.where(lo, w_l[c][:, h0:h0 + 1], w_l[c][:, h0 + 1:h0 + 2])
                y = ydiag + yoff[:, LANES * q:LANES * (q + 1)] * e_pair + dsk_ref[:, cols] * xp
                yg = y * _silu(z_scr[g, rows, LANES * q:LANES * (q + 1)])
                y_scr[rows, cols] = yg
                ssq[c] = ssq[c] + yg * yg
                xd_parts.append((xp * w_pair).astype(bf16))
            xd = jnp.concatenate(xd_parts, axis=1)
            cd_row = jnp.zeros((1, GW), f32)
            for r in range(4):
                cd_row = jnp.where(lane256 == r, cd_l[c][:, 4 * g + r:4 * g + r + 1], cd_row)
            s_g = s_g * cd_row + lax.dot_general(b_g, xd, TN_DIMS, preferred_element_type=f32)
        state_ref[g] = s_g

    cols = slice(GW * (ng - 1), GW * ng)
    for c in range(nch):
        rows = slice(SSM_CHUNK * c, SSM_CHUNK * (c + 1))
        part = _dot((y_scr[rows, cols] * gn_ref[:, cols]).astype(bf16), wout_ref[cols, :])
        rms = lax.rsqrt(jnp.sum(ssq[c], axis=-1, keepdims=True) * (1.0 / SSM_D_INNER) + EPS)
        out = (out_scr[rows, :] + part) * rms
        o_ref[0, rows, :] = x_ref[0, rows, :] + out * _rms_scale(out) * postw_ref[...]


def _ssd_layer(x, layer, j, pre_norm, post_norm, w_in, conv_w, conv_b, dt_bias, a_log, d_rep,
               gate_norm, w_out):
    bsz, seq, d = x.shape
    ts = SSD_TILE
    nt = seq // ts
    ng = SSM_GROUPS
    x_spec = pl.BlockSpec((1, ts, d), lambda b, t: (b, t, 0))
    zcol, xcol = 0, SSM_D_INNER // GW
    bcol = (2 * SSM_D_INNER) // LANES
    ccol = bcol + SSM_BC_DIM // LANES
    w_specs = ([_col_block_spec(j, d, GW, zcol + g) for g in range(ng)]
               + [_col_block_spec(j, d, GW, xcol + g) for g in range(ng)]
               + [_col_block_spec(j, d, LANES, bcol + g) for g in range(ng)]
               + [_col_block_spec(j, d, LANES, ccol + g) for g in range(ng)]
               + [_const_spec((d, SSM_HEADS))])
    wdt = w_in[j, :, SSM_D_INNER + SSM_CONV_DIM:]
    small = (conv_w, conv_b, dt_bias, a_log, d_rep, gate_norm, w_out)
    in_specs = ([x_spec, _layer_spec(pre_norm, layer), _layer_spec(post_norm, layer)] + w_specs
                + [_layer_spec(a, j) for a in small])
    return pl.pallas_call(
        _ssd_kernel,
        grid=(bsz, nt),
        in_specs=in_specs,
        out_specs=x_spec,
        out_shape=jax.ShapeDtypeStruct(x.shape, f32),
        scratch_shapes=[
            pltpu.VMEM((SSM_CONV_DIM // LANES, ts + SUBLANES, LANES), f32),
            pltpu.VMEM((ts, SSM_D_INNER), f32),
            pltpu.VMEM((ts, 2 * SSM_BC_DIM), bf16),
            pltpu.VMEM((ng, ts, GW), f32),
            pltpu.VMEM((ts, SSM_D_INNER), f32),
            pltpu.VMEM((ts, d), f32),
            pltpu.VMEM((ts, d), bf16),
            pltpu.VMEM((ng, SSM_STATE, 4 * SSM_HEAD_DIM), f32),
        ],
        compiler_params=pltpu.CompilerParams(
            dimension_semantics=("arbitrary", "arbitrary"),
            vmem_limit_bytes=VMEM_LIMIT_BYTES),
        name="ssd_layer",
    )(x, pre_norm, post_norm, *([w_in] * (4 * ng)), wdt, *small)


def _swa_kernel(*refs):
    nqw = ATT_WIDTH // GW
    sink_ref, x_ref, pos_ref, inv_ref, sel_ref, prew_ref, postw_ref = refs[:7]
    wq = refs[7:7 + nqw]
    wk_ref, wv_ref = refs[7 + nqw:9 + nqw]
    wg = refs[9 + nqw:9 + 2 * nqw]
    (wout_ref, o_ref, kpad, vtp, q_scr, o_scr, h_scr, s_scr, g_scr) = refs[9 + 2 * nqw:]
    ts = SWA_TILE
    nqb = ts // ATT_BLOCK
    t = pl.program_id(1)

    @pl.when(t == 0)
    def _():
        kpad[:, 0:ATT_BLOCK, :] = jnp.zeros((kpad.shape[0], ATT_BLOCK, LANES), bf16)
        vtp[...] = jnp.zeros(vtp.shape, bf16)

    x = x_ref[0]
    h_scr[...] = (x * _rms_scale(x) * prew_ref[...]).astype(bf16)
    h = h_scr[...]

    ang = inv_ref[...] * pos_ref[0].astype(f32)
    pieces = []
    for val in (jnp.cos(ang), jnp.sin(ang)):
        hi = val.astype(bf16).astype(f32)
        mid = (val - hi).astype(bf16).astype(f32)
        pieces += [hi, mid, (val - hi - mid).astype(bf16).astype(f32)]
    pieces.append(jnp.zeros((LANES - 8 * len(pieces), ts), f32))
    tabs = _dot(jnp.concatenate(pieces, axis=0).T.astype(bf16), sel_ref[...])
    l64 = lax.broadcasted_iota(jnp.int32, (ts, LANES), 1) % ATT_HEAD_DIM
    hi_half = lax.broadcasted_iota(jnp.int32, (ts, LANES), 1) >= ATT_HEAD_DIM
    cos_t = jnp.where(l64 < ROPE_DIM, tabs[:, :LANES], 1.0)
    sin_t = tabs[:, LANES:]

    def rope(tb, c, s):
        sw = jnp.where(l64 < ROPE_DIM // 2, pltpu.roll(tb, LANES - ROPE_DIM // 2, axis=1),
                       jnp.where(l64 < ROPE_DIM, pltpu.roll(tb, ROPE_DIM // 2, axis=1), 0.0))
        return tb * c + sw * s

    scale = ATT_HEAD_DIM ** -0.5 * LOG2E
    cos_q = cos_t * scale
    sin_q = sin_t * scale
    k = _dot(h, wk_ref[...])
    v = _dot(h, wv_ref[...])
    new = slice(ATT_BLOCK, ATT_BLOCK + ts)
    for kb in range(ATT_KV_WIDTH // LANES):
        kblk = rope(k[:, LANES * kb:LANES * (kb + 1)], cos_t, sin_t)
        ksw = pltpu.roll(kblk, ATT_HEAD_DIM, axis=1)
        vt = v[:, LANES * kb:LANES * (kb + 1)].T.astype(bf16)
        for hk in range(2):
            kh = 2 * kb + hk
            for hq in range(2):
                ksrc = kblk if hq == hk else ksw
                keep = hi_half if hq == 1 else jnp.logical_not(hi_half)
                kpad[2 * kh + hq, new, :] = jnp.where(keep, ksrc, 0.0).astype(bf16)
                vtp[2 * kh + hq, ATT_HEAD_DIM * hq:ATT_HEAD_DIM * (hq + 1), new] = (
                    vt[ATT_HEAD_DIM * hk:ATT_HEAD_DIM * (hk + 1), :])

    def project_q(jj):
        qb = _dot(h, wq[jj][...])
        for half in range(2):
            cols = slice(GW * jj + LANES * half, GW * jj + LANES * (half + 1))
            q_scr[:, cols] = rope(qb[:, LANES * half:LANES * (half + 1)], cos_q, sin_q).astype(bf16)

    def scores(jj):
        for j in (2 * jj, 2 * jj + 1):
            kh = j // 2
            for i in range(nqb):
                rows = slice(ATT_BLOCK * i, ATT_BLOCK * (i + 1))
                keys = slice(ATT_BLOCK * i, ATT_BLOCK * (i + 2))
                klhs = jnp.concatenate([kpad[2 * kh, keys, :], kpad[2 * kh + 1, keys, :]], axis=0)
                s_scr[(ATT_WIDTH // LANES) * i + j] = lax.dot_general(
                    klhs, q_scr[rows, LANES * j:LANES * (j + 1)], NT_DIMS,
                    preferred_element_type=f32)

    def project_gate(jj):
        g_scr[:, GW * jj:GW * (jj + 1)] = _silu(_dot(h, wg[jj][...]))

    project_q(0)
    for jj in range(1, nqw):
        project_q(jj)
        scores(jj - 1)
    project_gate(0)
    scores(nqw - 1)
    for jj in range(1, nqw):
        project_gate(jj)

    kj = lax.broadcasted_iota(jnp.int32, (2 * ATT_BLOCK, ATT_BLOCK), 0)
    qi = lax.broadcasted_iota(jnp.int32, (2 * ATT_BLOCK, ATT_BLOCK), 1)
    dist = qi + ATT_BLOCK - kj
    in_window = jnp.logical_and(dist >= 0, dist < ATT_BLOCK)
    first_ok = jnp.logical_and(in_window, jnp.logical_or(kj >= ATT_BLOCK, t > 0))
    low_rows = lax.broadcasted_iota(jnp.int32, (LANES, ATT_BLOCK), 0) < ATT_HEAD_DIM
    for i in range(nqb):
        rows = slice(ATT_BLOCK * i, ATT_BLOCK * (i + 1))
        keys = slice(ATT_BLOCK * i, ATT_BLOCK * (i + 2))
        valid = first_ok if i == 0 else in_window
        for j in range(ATT_WIDTH // LANES):
            kh = j // 2
            ps, rdens = [], []
            for hq in range(2):
                sink2 = sink_ref[2 * j + hq] * LOG2E
                s = s_scr[(ATT_WIDTH // LANES) * i + j, 2 * ATT_BLOCK * hq:2 * ATT_BLOCK * (hq + 1), :]
                s = jnp.where(valid, s, -jnp.inf)
                m = jnp.maximum(jnp.max(s, axis=0, keepdims=True), sink2)
                p = jnp.exp2(s - m)
                rdens.append(1.0 / (jnp.sum(p, axis=0, keepdims=True) + jnp.exp2(sink2 - m)))
                ps.append(p.astype(bf16))
            vcat = jnp.concatenate([vtp[2 * kh, :, keys], vtp[2 * kh + 1, :, keys]], axis=1)
            ot = _dot(vcat, jnp.concatenate(ps, axis=0))
            ot = ot * jnp.where(low_rows, rdens[0], rdens[1])
            o_scr[rows, LANES * j:LANES * (j + 1)] = ot.T

    kpad[:, 0:ATT_BLOCK, :] = kpad[:, ts:ts + ATT_BLOCK, :]
    vtp[:, :, 0:ATT_BLOCK] = vtp[:, :, ts:ts + ATT_BLOCK]

    for i in range(nqb):
        rows = slice(ATT_BLOCK * i, ATT_BLOCK * (i + 1))
        out = _dot((o_scr[rows, :] * g_scr[rows, :]).astype(bf16), wout_ref[...])
        o_ref[0, rows, :] = x_ref[0, rows, :] + out * _rms_scale(out) * postw_ref[...]


def _rope_selector():
    sel = np.zeros((LANES, 2 * LANES), np.float32)
    half = ROPE_DIM // 2
    for r in range(6 * half):
        kind, f = r // (3 * half), r % half
        for lane in range(LANES):
            d = lane % ATT_HEAD_DIM
            if d < ROPE_DIM and d % half == f:
                sel[r, LANES * kind + lane] = -1.0 if (kind == 1 and d < half) else 1.0
    return jnp.asarray(sel, bf16)


def _swa_layer(x, positions, layer, j, pre_norm, post_norm, w_in, sinks, w_out):
    bsz, seq, d = x.shape
    ts = SWA_TILE
    nt = seq // ts
    nqw = ATT_WIDTH // GW
    x_spec = pl.BlockSpec((1, ts, d), lambda b, t: (b, t, 0))
    pos_spec = pl.BlockSpec((1, 1, ts), lambda b, t: (b * nt + t, 0, 0))
    kcol = ATT_WIDTH // GW
    gcol = (ATT_WIDTH + 2 * ATT_KV_WIDTH) // GW
    w_specs = ([_col_block_spec(j, d, GW, jj) for jj in range(nqw)]
               + [_col_block_spec(j, d, GW, kcol), _col_block_spec(j, d, GW, kcol + 1)]
               + [_col_block_spec(j, d, GW, gcol + jj) for jj in range(nqw)])
    inv = ROPE_THETA ** (-jnp.arange(0, ROPE_DIM, 2, dtype=f32) / ROPE_DIM)
    consts = (jnp.broadcast_to(inv[:, None], (ROPE_DIM // 2, ts)), _rope_selector())
    in_specs = ([pl.BlockSpec(memory_space=pltpu.SMEM), x_spec, pos_spec]
                + [_const_spec(a.shape) for a in consts]
                + [_layer_spec(pre_norm, layer), _layer_spec(post_norm, layer)] + w_specs
                + [_layer_spec(w_out, j)])
    return pl.pallas_call(
        _swa_kernel,
        grid=(bsz, nt),
        in_specs=in_specs,
        out_specs=x_spec,
        out_shape=jax.ShapeDtypeStruct(x.shape, f32),
        scratch_shapes=[
            pltpu.VMEM((2 * ATT_KV_HEADS, ts + ATT_BLOCK, LANES), bf16),
            pltpu.VMEM((2 * ATT_KV_HEADS, LANES, ts + ATT_BLOCK), bf16),
            pltpu.VMEM((ts, ATT_WIDTH), bf16),
            pltpu.VMEM((ts, ATT_WIDTH), f32),
            pltpu.VMEM((ts, d), bf16),
            pltpu.VMEM((ATT_WIDTH // LANES * ts // ATT_BLOCK, 4 * ATT_BLOCK, ATT_BLOCK), f32),
            pltpu.VMEM((ts, ATT_WIDTH), f32),
        ],
        compiler_params=pltpu.CompilerParams(
            dimension_semantics=("arbitrary", "arbitrary"),
            vmem_limit_bytes=VMEM_LIMIT_BYTES),
        name="swa_layer",
    )(sinks[j], x, positions.reshape(bsz * nt, 1, ts), *consts, pre_norm, post_norm,
      *([w_in] * (2 * nqw + 2)), w_out)


def kernel(x, positions, pre_norm, post_norm, ssm_w_in, ssm_conv_w, ssm_conv_b, ssm_dt_bias, ssm_a_log, ssm_d, ssm_gate_norm, ssm_w_out, att_w_in, att_sinks, att_w_out):
    depth = pre_norm.shape[0]

    def rows(a):
        return a.reshape(a.shape[0], 1, a.shape[1])

    pre_norm, post_norm = rows(pre_norm), rows(post_norm)
    ssm = (ssm_w_in.astype(bf16), ssm_conv_w, rows(ssm_conv_b), rows(ssm_dt_bias), rows(ssm_a_log),
           rows(jnp.repeat(ssm_d, SSM_HEAD_DIM, axis=1)), rows(ssm_gate_norm), ssm_w_out.astype(bf16))
    att_w_in, att_w_out = att_w_in.astype(bf16), att_w_out.astype(bf16)
    for i in range(depth):
        j = i // 2
        if i % 2 == 0:
            x = _ssd_layer(x, i, j, pre_norm, post_norm, *ssm)
        else:
            x = _swa_layer(x, positions, i, j, pre_norm, post_norm, att_w_in, att_sinks, att_w_out)
    return x
```

```python
import jax
import jax.numpy as jnp
import numpy as np
from jax import lax
from jax.experimental import pallas as pl
from jax.experimental.pallas import tpu as pltpu

f32 = jnp.float32
bf16 = jnp.bfloat16

EPS = 1e-6
LOG2E = 1.4426950408889634
LANES = 128
SUBLANES = 8
VMEM_LIMIT_BYTES = 56 * 1024 * 1024

D_MODEL = 1024
SSM_D_INNER = 2048
SSM_HEAD_DIM = 64
SSM_HEADS = 32
SSM_GROUPS = 8
SSM_STATE = 128
SSM_CONV = 4
SSM_CHUNK = 128
SSM_BC_DIM = SSM_GROUPS * SSM_STATE
SSM_CONV_DIM = SSM_D_INNER + 2 * SSM_BC_DIM
ATT_HEAD_DIM = 64
ATT_Q_HEADS = 16
ATT_KV_HEADS = 4
ATT_WIDTH = 1024
ATT_KV_WIDTH = 256
ATT_BLOCK = 128
ROPE_THETA = 500000.0
ROPE_DIM = 16

SSD_TILE = 256
SWA_TILE = 512
GW = 2 * LANES

NT_DIMS = (((1,), (1,)), ((), ()))
TN_DIMS = (((0,), (0,)), ((), ()))


def _dot(a, b):
    return jnp.dot(a, b, preferred_element_type=f32)


def _rms_scale(v):
    return lax.rsqrt(jnp.sum(v * v, axis=-1, keepdims=True) * (1.0 / v.shape[-1]) + EPS)


def _silu(v):
    return v * jax.nn.sigmoid(v)


def _const_spec(shape):
    nd = len(shape)
    return pl.BlockSpec(shape, lambda b, t: (0,) * nd, pipeline_mode=pl.Buffered(1))


def _layer_spec(arr, layer):
    rest = arr.shape[1:]
    return pl.BlockSpec((None,) + rest, lambda b, t: (layer,) + (0,) * len(rest),
                        pipeline_mode=pl.Buffered(1))


def _col_block_spec(layer, rows, width, index):
    return pl.BlockSpec((None, rows, width), lambda b, t: (layer, 0, index),
                        pipeline_mode=pl.Buffered(1))


def _split3(v):
    hi = v.astype(bf16)
    r1 = v - hi.astype(f32)
    mid = r1.astype(bf16)
    return hi, mid, (r1 - mid.astype(f32)).astype(bf16)


def _ssd_kernel(*refs):
    ng = SSM_GROUPS
    x_ref, prew_ref, postw_ref = refs[:3]
    wz, wxs, wb, wc = (refs[3 + ng * i:3 + ng * (i + 1)] for i in range(4))
    (wdt_ref, cw_ref, cb_ref, dtb_ref, alog_ref, dsk_ref, gn_ref, wout_ref, o_ref,
     cbuf, xs_scr, bc_scr, z_scr, y_scr, out_scr, h_scr, state_ref) = refs[3 + 4 * ng:]
    ts = SSD_TILE
    nch = ts // SSM_CHUNK
    t = pl.program_id(1)

    @pl.when(t == 0)
    def _():
        cbuf[:, 0:SUBLANES, :] = jnp.zeros((cbuf.shape[0], SUBLANES, LANES), f32)
        state_ref[...] = jnp.zeros(state_ref.shape, f32)

    x = x_ref[0]
    h_scr[...] = (x * _rms_scale(x) * prew_ref[...]).astype(bf16)

    def conv_cols(g, part, half):
        if part == 0:
            start = GW * g + LANES * half
        else:
            start = SSM_D_INNER + SSM_BC_DIM * half + LANES * g
        return slice(start, start + LANES)

    def project(g):
        h = h_scr[...]
        raws = (_dot(h, wxs[g][...]),
                _dot(h, jnp.concatenate([wb[g][...], wc[g][...]], axis=1)))
        for part in range(2):
            for half in range(2):
                cbuf[4 * g + 2 * part + half, SUBLANES:SUBLANES + ts, :] = (
                    raws[part][:, LANES * half:LANES * (half + 1)])
        z_scr[g] = _dot(h, wz[g][...])

    def conv(g):
        for part in range(2):
            for half in range(2):
                i = 4 * g + 2 * part + half
                cols = conv_cols(g, part, half)
                acc = cb_ref[:, cols] + cbuf[i, pl.ds(SUBLANES - 3, ts), :] * cw_ref[0:1, cols]
                for k in range(1, SSM_CONV):
                    acc = acc + cbuf[i, pl.ds(SUBLANES - 3 + k, ts), :] * cw_ref[k:k + 1, cols]
                act = _silu(acc)
                cbuf[i, 0:SUBLANES, :] = cbuf[i, ts:ts + SUBLANES, :]
                dst = slice(GW * g + LANES * half, GW * g + LANES * (half + 1))
                if part == 0:
                    xs_scr[:, dst] = act
                else:
                    bc_scr[:, dst] = act.astype(bf16)

    def out_project(g):
        cols = slice(GW * g, GW * (g + 1))
        part = _dot((y_scr[:, cols] * gn_ref[:, cols]).astype(bf16), wout_ref[cols, :])
        if g == 0:
            out_scr[...] = part
        else:
            out_scr[...] += part

    dt = jax.nn.softplus(_dot(h_scr[...], wdt_ref[...]) + dtb_ref[...])
    a = dt * (-jnp.exp(alog_ref[...]))
    project(0)
    conv(0)
    project(1)

    r_i = lax.broadcasted_iota(jnp.int32, (SSM_CHUNK, SSM_CHUNK), 0)
    c_i = lax.broadcasted_iota(jnp.int32, (SSM_CHUNK, SSM_CHUNK), 1)
    causal = r_i >= c_i
    ltri = causal.astype(bf16)
    ltri3 = jnp.concatenate([ltri, ltri, ltri], axis=1)
    lo = c_i < SSM_HEAD_DIM
    acs2_l, rowt_l, eacs_l, w_l, cd_l = [], [], [], [], []
    for c in range(nch):
        rows = slice(SSM_CHUNK * c, SSM_CHUNK * (c + 1))
        acs = _dot(ltri3, jnp.concatenate(_split3(a[rows]), axis=0))
        last = acs[SSM_CHUNK - 1:SSM_CHUNK, :]
        acs2 = acs * LOG2E
        acs2_l.append(acs2)
        rowt_l.append((acs2 - jnp.log(dt[rows]) * LOG2E).T)
        eacs_l.append(jnp.exp(acs))
        w_l.append(dt[rows] * jnp.exp(last - acs))
        cd_l.append(jnp.exp(last))

    lane256 = lax.broadcasted_iota(jnp.int32, (1, GW), 1) // SSM_HEAD_DIM
    ssq = [jnp.zeros((SSM_CHUNK, LANES), f32) for _ in range(nch)]
    for g in range(ng):
        s_g = state_ref[g]
        cbms = []
        for c in range(nch):
            rows = slice(SSM_CHUNK * c, SSM_CHUNK * (c + 1))
            cbms.append(lax.dot_general(bc_scr[rows, GW * g + LANES:GW * (g + 1)],
                                        bc_scr[rows, GW * g:GW * g + LANES],
                                        NT_DIMS, preferred_element_type=f32))
        if g + 1 < ng:
            conv(g + 1)
        if g + 2 < ng:
            project(g + 2)
        for c in range(nch):
            rows = slice(SSM_CHUNK * c, SSM_CHUNK * (c + 1))
            b_g = bc_scr[rows, GW * g:GW * g + LANES]
            c_g = bc_scr[rows, GW * g + LANES:GW * (g + 1)]
            cbm = cbms[c]
            if c == nch - 1 and g > 0:
                out_project(g - 1)
            yoff = _dot(c_g, s_g.astype(bf16))
            xd_parts = []
            for q in range(2):
                h0 = 4 * g + 2 * q
                cols = slice(GW * g + LANES * q, GW * g + LANES * (q + 1))
                xp = xs_scr[rows, cols]
                xpb = xp.astype(bf16)
                ms = []
                for hh in (h0, h0 + 1):
                    seg2 = acs2_l[c][:, hh:hh + 1] - rowt_l[c][hh:hh + 1, :]
                    ms.append((cbm * jnp.exp2(jnp.where(causal, seg2, -jnp.inf))).astype(bf16))
                zb = jnp.zeros_like(xpb)
                bd = jnp.concatenate([jnp.where(lo, xpb, zb), jnp.where(lo, zb, xpb)], axis=0)
                ydiag = _dot(jnp.concatenate(ms, axis=1), bd)
                e_pair = jnp.where(lo, eacs_l[c][:, h0:h0 + 1], eacs_l[c][:, h0 + 1:h0 + 2])
                w_pair = jnp.where(lo, w_l[c][:, h0:h0 + 1], w_l[c][:, h0 + 1:h0 + 2])
                y = ydiag + yoff[:, LANES * q:LANES * (q + 1)] * e_pair + dsk_ref[:, cols] * xp
                yg = y * _silu(z_scr[g, rows, LANES * q:LANES * (q + 1)])
                y_scr[rows, cols] = yg
                ssq[c] = ssq[c] + yg * yg
                xd_parts.append((xp * w_pair).astype(bf16))
            xd = jnp.concatenate(xd_parts, axis=1)
            cd_row = jnp.zeros((1, GW), f32)
            for r in range(4):
                cd_row = jnp.where(lane256 == r, cd_l[c][:, 4 * g + r:4 * g + r + 1], cd_row)
            s_g = s_g * cd_row + lax.dot_general(b_g, xd, TN_DIMS, preferred_element_type=f32)
        state_ref[g] = s_g

    cols = slice(GW * (ng - 1), GW * ng)
    for c in range(nch):
        rows = slice(SSM_CHUNK * c, SSM_CHUNK * (c + 1))
        part = _dot((y_scr[rows, cols] * gn_ref[:, cols]).astype(bf16), wout_ref[cols, :])
        rms = lax.rsqrt(jnp.sum(ssq[c], axis=-1, keepdims=True) * (1.0 / SSM_D_INNER) + EPS)
        out = (out_scr[rows, :] + part) * rms
        o_ref[0, rows, :] = x_ref[0, rows, :] + out * _rms_scale(out) * postw_ref[...]


def _ssd_layer(x, layer, j, pre_norm, post_norm, w_in, conv_w, conv_b, dt_bias, a_log, d_rep,
               gate_norm, w_out):
    bsz, seq, d = x.shape
    ts = SSD_TILE
    nt = seq // ts
    ng = SSM_GROUPS
    x_spec = pl.BlockSpec((1, ts, d), lambda b, t: (b, t, 0))
    zcol, xcol = 0, SSM_D_INNER // GW
    bcol = (2 * SSM_D_INNER) // LANES
    ccol = bcol + SSM_BC_DIM // LANES
    w_specs = ([_col_block_spec(j, d, GW, zcol + g) for g in range(ng)]
               + [_col_block_spec(j, d, GW, xcol + g) for g in range(ng)]
               + [_col_block_spec(j, d, LANES, bcol + g) for g in range(ng)]
               + [_col_block_spec(j, d, LANES, ccol + g) for g in range(ng)]
               + [_const_spec((d, SSM_HEADS))])
    wdt = w_in[j, :, SSM_D_INNER + SSM_CONV_DIM:]
    small = (conv_w, conv_b, dt_bias, a_log, d_rep, gate_norm, w_out)
    in_specs = ([x_spec, _layer_spec(pre_norm, layer), _layer_spec(post_norm, layer)] + w_specs
                + [_layer_spec(a, j) for a in small])
    return pl.pallas_call(
        _ssd_kernel,
        grid=(bsz, nt),
        in_specs=in_specs,
        out_specs=x_spec,
        out_shape=jax.ShapeDtypeStruct(x.shape, f32),
        scratch_shapes=[
            pltpu.VMEM((SSM_CONV_DIM // LANES, ts + SUBLANES, LANES), f32),
            pltpu.VMEM((ts, SSM_D_INNER), f32),
            pltpu.VMEM((ts, 2 * SSM_BC_DIM), bf16),
            pltpu.VMEM((ng, ts, GW), f32),
            pltpu.VMEM((ts, SSM_D_INNER), f32),
            pltpu.VMEM((ts, d), f32),
            pltpu.VMEM((ts, d), bf16),
            pltpu.VMEM((ng, SSM_STATE, 4 * SSM_HEAD_DIM), f32),
        ],
        compiler_params=pltpu.CompilerParams(
            dimension_semantics=("arbitrary", "arbitrary"),
            vmem_limit_bytes=VMEM_LIMIT_BYTES),
        name="ssd_layer",
    )(x, pre_norm, post_norm, *([w_in] * (4 * ng)), wdt, *small)


def _swa_kernel(*refs):
    nqw = ATT_WIDTH // GW
    sink_ref, x_ref, pos_ref, inv_ref, sel_ref, prew_ref, postw_ref = refs[:7]
    wq = refs[7:7 + nqw]
    wk_ref, wv_ref = refs[7 + nqw:9 + nqw]
    wg = refs[9 + nqw:9 + 2 * nqw]
    (wout_ref, o_ref, kpad, vtp, q_scr, o_scr, h_scr, s_scr, g_scr) = refs[9 + 2 * nqw:]
    ts = SWA_TILE
    nqb = ts // ATT_BLOCK
    t = pl.program_id(1)

    @pl.when(t == 0)
    def _():
        kpad[:, 0:ATT_BLOCK, :] = jnp.zeros((kpad.shape[0], ATT_BLOCK, LANES), bf16)
        vtp[...] = jnp.zeros(vtp.shape, bf16)

    x = x_ref[0]
    h_scr[...] = (x * _rms_scale(x) * prew_ref[...]).astype(bf16)
    h = h_scr[...]
    k = _dot(h, wk_ref[...])
    v = _dot(h, wv_ref[...])

    ang = inv_ref[...] * pos_ref[0].astype(f32)
    pieces = []
    for val in (jnp.cos(ang), jnp.sin(ang)):
        hi = val.astype(bf16).astype(f32)
        mid = (val - hi).astype(bf16).astype(f32)
        pieces += [hi, mid, (val - hi - mid).astype(bf16).astype(f32)]
    pieces.append(jnp.zeros((LANES - 8 * len(pieces), ts), f32))
    tabs = _dot(jnp.concatenate(pieces, axis=0).T.astype(bf16), sel_ref[...])
    l64 = lax.broadcasted_iota(jnp.int32, (ts, LANES), 1) % ATT_HEAD_DIM
    hi_half = lax.broadcasted_iota(jnp.int32, (ts, LANES), 1) >= ATT_HEAD_DIM
    cos_t = jnp.where(l64 < ROPE_DIM, tabs[:, :LANES], 1.0)
    sin_t = tabs[:, LANES:]

    def rope(tb, c, s):
        sw = jnp.where(l64 < ROPE_DIM // 2, pltpu.roll(tb, LANES - ROPE_DIM // 2, axis=1),
                       jnp.where(l64 < ROPE_DIM, pltpu.roll(tb, ROPE_DIM // 2, axis=1), 0.0))
        return tb * c + sw * s

    scale = ATT_HEAD_DIM ** -0.5 * LOG2E
    cos_q = cos_t * scale
    sin_q = sin_t * scale
    new = slice(ATT_BLOCK, ATT_BLOCK + ts)
    for kb in range(ATT_KV_WIDTH // LANES):
        kblk = rope(k[:, LANES * kb:LANES * (kb + 1)], cos_t, sin_t)
        ksw = pltpu.roll(kblk, ATT_HEAD_DIM, axis=1)
        vt = v[:, LANES * kb:LANES * (kb + 1)].T.astype(bf16)
        for hk in range(2):
            kh = 2 * kb + hk
            for hq in range(2):
                ksrc = kblk if hq == hk else ksw
                keep = hi_half if hq == 1 else jnp.logical_not(hi_half)
                kpad[2 * kh + hq, new, :] = jnp.where(keep, ksrc, 0.0).astype(bf16)
                vtp[2 * kh + hq, ATT_HEAD_DIM * hq:ATT_HEAD_DIM * (hq + 1), new] = (
                    vt[ATT_HEAD_DIM * hk:ATT_HEAD_DIM * (hk + 1), :])

    def project_q(jj):
        qb = _dot(h, wq[jj][...])
        for half in range(2):
            cols = slice(GW * jj + LANES * half, GW * jj + LANES * (half + 1))
            q_scr[:, cols] = rope(qb[:, LANES * half:LANES * (half + 1)], cos_q, sin_q).astype(bf16)

    def scores(jj):
        for j in (2 * jj, 2 * jj + 1):
            kh = j // 2
            for i in range(nqb):
                rows = slice(ATT_BLOCK * i, ATT_BLOCK * (i + 1))
                keys = slice(ATT_BLOCK * i, ATT_BLOCK * (i + 2))
                klhs = jnp.concatenate([kpad[2 * kh, keys, :], kpad[2 * kh + 1, keys, :]], axis=0)
                s_scr[(ATT_WIDTH // LANES) * i + j] = lax.dot_general(
                    klhs, q_scr[rows, LANES * j:LANES * (j + 1)], NT_DIMS,
                    preferred_element_type=f32)

    def project_gate(jj):
        g_scr[:, GW * jj:GW * (jj + 1)] = _silu(_dot(h, wg[jj][...]))

    project_q(0)
    for jj in range(1, nqw):
        project_q(jj)
        scores(jj - 1)
    project_gate(0)
    scores(nqw - 1)
    for jj in range(1, nqw):
        project_gate(jj)

    kj = lax.broadcasted_iota(jnp.int32, (2 * ATT_BLOCK, ATT_BLOCK), 0)
    qi = lax.broadcasted_iota(jnp.int32, (2 * ATT_BLOCK, ATT_BLOCK), 1)
    dist = qi + ATT_BLOCK - kj
    in_window = jnp.logical_and(dist >= 0, dist < ATT_BLOCK)
    first_ok = jnp.logical_and(in_window, jnp.logical_or(kj >= ATT_BLOCK, t > 0))
    low_rows = lax.broadcasted_iota(jnp.int32, (LANES, ATT_BLOCK), 0) < ATT_HEAD_DIM
    for i in range(nqb):
        rows = slice(ATT_BLOCK * i, ATT_BLOCK * (i + 1))
        keys = slice(ATT_BLOCK * i, ATT_BLOCK * (i + 2))
        valid = first_ok if i == 0 else in_window
        for j in range(ATT_WIDTH // LANES):
            kh = j // 2
            ps, rdens = [], []
            for hq in range(2):
                sink2 = sink_ref[2 * j + hq] * LOG2E
                s = s_scr[(ATT_WIDTH // LANES) * i + j, 2 * ATT_BLOCK * hq:2 * ATT_BLOCK * (hq + 1), :]
                s = jnp.where(valid, s, -jnp.inf)
                m = jnp.maximum(jnp.max(s, axis=0, keepdims=True), sink2)
                p = jnp.exp2(s - m)
                rdens.append(1.0 / (jnp.sum(p, axis=0, keepdims=True) + jnp.exp2(sink2 - m)))
                ps.append(p.astype(bf16))
            vcat = jnp.concatenate([vtp[2 * kh, :, keys], vtp[2 * kh + 1, :, keys]], axis=1)
            ot = _dot(vcat, jnp.concatenate(ps, axis=0))
            ot = ot * jnp.where(low_rows, rdens[0], rdens[1])
            o_scr[rows, LANES * j:LANES * (j + 1)] = ot.T

    kpad[:, 0:ATT_BLOCK, :] = kpad[:, ts:ts + ATT_BLOCK, :]
    vtp[:, :, 0:ATT_BLOCK] = vtp[:, :, ts:ts + ATT_BLOCK]

    for i in range(nqb):
        rows = slice(ATT_BLOCK * i, ATT_BLOCK * (i + 1))
        out = _dot((o_scr[rows, :] * g_scr[rows, :]).astype(bf16), wout_ref[...])
        o_ref[0, rows, :] = x_ref[0, rows, :] + out * _rms_scale(out) * postw_ref[...]


def _rope_selector():
    sel = np.zeros((LANES, 2 * LANES), np.float32)
    half = ROPE_DIM // 2
    for r in range(6 * half):
        kind, f = r // (3 * half), r % half
        for lane in range(LANES):
            d = lane % ATT_HEAD_DIM
            if d < ROPE_DIM and d % half == f:
                sel[r, LANES * kind + lane] = -1.0 if (kind == 1 and d < half) else 1.0
    return jnp.asarray(sel, bf16)


def _swa_layer(x, positions, layer, j, pre_norm, post_norm, w_in, sinks, w_out):
    bsz, seq, d = x.shape
    ts = SWA_TILE
    nt = seq // ts
    nqw = ATT_WIDTH // GW
    x_spec = pl.BlockSpec((1, ts, d), lambda b, t: (b, t, 0))
    pos_spec = pl.BlockSpec((1, 1, ts), lambda b, t: (b * nt + t, 0, 0))
    kcol = ATT_WIDTH // GW
    gcol = (ATT_WIDTH + 2 * ATT_KV_WIDTH) // GW
    w_specs = ([_col_block_spec(j, d, GW, jj) for jj in range(nqw)]
               + [_col_block_spec(j, d, GW, kcol), _col_block_spec(j, d, GW, kcol + 1)]
               + [_col_block_spec(j, d, GW, gcol + jj) for jj in range(nqw)])
    inv = ROPE_THETA ** (-jnp.arange(0, ROPE_DIM, 2, dtype=f32) / ROPE_DIM)
    consts = (jnp.broadcast_to(inv[:, None], (ROPE_DIM // 2, ts)), _rope_selector())
    in_specs = ([pl.BlockSpec(memory_space=pltpu.SMEM), x_spec, pos_spec]
                + [_const_spec(a.shape) for a in consts]
                + [_layer_spec(pre_norm, layer), _layer_spec(post_norm, layer)] + w_specs
                + [_layer_spec(w_out, j)])
    return pl.pallas_call(
        _swa_kernel,
        grid=(bsz, nt),
        in_specs=in_specs,
        out_specs=x_spec,
        out_shape=jax.ShapeDtypeStruct(x.shape, f32),
        scratch_shapes=[
            pltpu.VMEM((2 * ATT_KV_HEADS, ts + ATT_BLOCK, LANES), bf16),
            pltpu.VMEM((2 * ATT_KV_HEADS, LANES, ts + ATT_BLOCK), bf16),
            pltpu.VMEM((ts, ATT_WIDTH), bf16),
            pltpu.VMEM((ts, ATT_WIDTH), f32),
            pltpu.VMEM((ts, d), bf16),
            pltpu.VMEM((ATT_WIDTH // LANES * ts // ATT_BLOCK, 4 * ATT_BLOCK, ATT_BLOCK), f32),
            pltpu.VMEM((ts, ATT_WIDTH), f32),
        ],
        compiler_params=pltpu.CompilerParams(
            dimension_semantics=("arbitrary", "arbitrary"),
            vmem_limit_bytes=VMEM_LIMIT_BYTES),
        name="swa_layer",
    )(sinks[j], x, positions.reshape(bsz * nt, 1, ts), *consts, pre_norm, post_norm,
      *([w_in] * (2 * nqw + 2)), w_out)


def kernel(x, positions, pre_norm, post_norm, ssm_w_in, ssm_conv_w, ssm_conv_b, ssm_dt_bias, ssm_a_log, ssm_d, ssm_gate_norm, ssm_w_out, att_w_in, att_sinks, att_w_out):
    depth = pre_norm.shape[0]

    def rows(a):
        return a.reshape(a.shape[0], 1, a.shape[1])

    pre_norm, post_norm = rows(pre_norm), rows(post_norm)
    ssm = (ssm_w_in.astype(bf16), ssm_conv_w, rows(ssm_conv_b), rows(ssm_dt_bias), rows(ssm_a_log),
           rows(jnp.repeat(ssm_d, SSM_HEAD_DIM, axis=1)), rows(ssm_gate_norm), ssm_w_out.astype(bf16))
    att_w_in, att_w_out = att_w_in.astype(bf16), att_w_out.astype(bf16)
    for i in range(depth):
        j = i // 2
        if i % 2 == 0:
            x = _ssd_layer(x, i, j, pre_norm, post_norm, *ssm)
        else:
            x = _swa_layer(x, positions, i, j, pre_norm, post_norm, att_w_in, att_sinks, att_w_out)
    return x
```
